```python
import jax, jax.numpy as jnp
from jax import lax
import numpy as np

D_MODEL = 1024
BATCH = 8
SEQ = 2048
DEPTH = 1
DEC_BATCH = 2
DEC_SEQ = 16384
PAST_LEN = 128

GRID_W = 64
EPS = 1e-6
N_HEADS = 8
N_KV_HEADS = 2
HEAD_DIM = 64
D_ATTN = N_HEADS * HEAD_DIM
D_KV = N_KV_HEADS * HEAD_DIM
Q_BLOCK = 128
ROPE_THETA = 10000.0
SSM_HEADS = 8
SSM_HEAD_DIM = 64
D_SSM = SSM_HEADS * SSM_HEAD_DIM
SSM_STATE = 64
SSM_GROUPS = 2
CONV_K = 5
CHUNK = 128
D_XBC = D_SSM + 2 * SSM_GROUPS * SSM_STATE
D_MIX = D_ATTN + D_SSM
D_IN_PROJ = D_ATTN + 2 * D_KV + D_SSM + D_XBC + 2 * SSM_HEADS
N_EGROUPS = 4
EXPERTS_PER_GROUP = 8
N_EXPERTS = N_EGROUPS * EXPERTS_PER_GROUP
TOP_K = 2
D_EXPERT = 512
MOE_BLOCK = 128
N_MOD = 6

kernel_name = "hymba_axial_gqa_bissd_hmoe_encoder"


def rmsnorm(x, g):
    xf = x.astype(jnp.float32)
    y = xf * lax.rsqrt(jnp.mean(xf * xf, axis=-1, keepdims=True) + EPS) * g.astype(jnp.float32)
    return y.astype(x.dtype)


def rope_axis(x, pos):
    half = x.shape[-1] // 2
    inv = ROPE_THETA ** (-jnp.arange(half, dtype=jnp.float32) / half)
    ang = pos.astype(jnp.float32)[:, None] * inv[None, :]
    cos = jnp.cos(ang)[None, :, None, :]
    sin = jnp.sin(ang)[None, :, None, :]
    x1, x2 = x[..., :half], x[..., half:]
    return jnp.concatenate([x1 * cos - x2 * sin, x2 * cos + x1 * sin], axis=-1)


def axial_rope(x, row_idx, col_idx):
    h = x.shape[-1] // 2
    return jnp.concatenate([rope_axis(x[..., :h], row_idx), rope_axis(x[..., h:], col_idx)], axis=-1)


def grid_attention(q, k, v, q_gain, k_gain):
    B, S, _ = q.shape
    rows = S // GRID_W
    row_idx = jnp.repeat(jnp.arange(rows), GRID_W)
    col_idx = jnp.arange(S) % GRID_W
    G = N_HEADS // N_KV_HEADS
    qf = rmsnorm(q.reshape(B, S, N_HEADS, HEAD_DIM).astype(jnp.float32), q_gain)
    kf = rmsnorm(k.reshape(B, S, N_KV_HEADS, HEAD_DIM).astype(jnp.float32), k_gain)
    qf = axial_rope(qf, row_idx, col_idx) * (HEAD_DIM ** -0.5)
    kf = axial_rope(kf, row_idx, col_idx)
    qh = qf.astype(v.dtype)
    kh = kf.astype(v.dtype)
    vh = v.reshape(B, S, N_KV_HEADS, HEAD_DIM)
    nblk = S // Q_BLOCK
    qb = qh.reshape(B, nblk, Q_BLOCK, N_KV_HEADS, G, HEAD_DIM).transpose(1, 0, 3, 4, 2, 5)

    def one_block(qi):
        s = jnp.einsum('bkgqd,bskd->bkgqs', qi, kh).astype(jnp.float32)
        p = jax.nn.softmax(s, axis=-1).astype(vh.dtype)
        return jnp.einsum('bkgqs,bskd->bqkgd', p, vh)

    o = lax.map(one_block, qb)
    return o.transpose(1, 0, 2, 3, 4, 5).reshape(B, S, D_ATTN)


def ssd_scan(x, dt, A, Bm, Cm):
    b, s, h, p = x.shape
    n = Bm.shape[-1]
    c = s // CHUNK
    x = x.reshape(b, c, CHUNK, h, p)
    dt = dt.reshape(b, c, CHUNK, h)
    Bm = Bm.reshape(b, c, CHUNK, h, n)
    Cm = Cm.reshape(b, c, CHUNK, h, n)
    a_cs = jnp.cumsum(dt * A, axis=2)
    seg = a_cs[:, :, :, None, :] - a_cs[:, :, None, :, :]
    mask = jnp.tril(jnp.ones((CHUNK, CHUNK), dtype=bool))[:, :, None]
    Lmat = jnp.exp(jnp.where(mask, seg, -jnp.inf))
    xdt = x * dt[..., None]
    scores = jnp.einsum('bclhn,bcshn->bclsh', Cm, Bm) * Lmat
    y_diag = jnp.einsum('bclsh,bcshp->bclhp', scores, xdt)
    decay_to_end = jnp.exp(a_cs[:, :, -1:, :] - a_cs)
    states = jnp.einsum('bclhn,bclh,bclhp->bchpn', Bm, decay_to_end, xdt)
    chunk_decay = jnp.exp(a_cs[:, :, -1, :])

    def step(carry, inp):
        dec, st = inp
        return dec[:, :, None, None] * carry + st, carry

    _, prev = lax.scan(step, jnp.zeros((b, h, p, n), jnp.float32),
                       (chunk_decay.transpose(1, 0, 2), states.transpose(1, 0, 2, 3, 4)))
    prev = prev.transpose(1, 0, 2, 3, 4)
    y_off = jnp.einsum('bclhn,bchpn,bclh->bclhp', Cm, prev, jnp.exp(a_cs))
    return (y_diag + y_off).reshape(b, s, h, p)


def ssd_mixer(z, xbc, dt_raw, conv_w, conv_b, a_log, dt_bias, d_skip, norm_g):
    B, S, _ = xbc.shape
    pad = CONV_K // 2
    xbc = lax.conv_general_dilated(xbc, conv_w, window_strides=(1,), padding=[(pad, pad)],
                                   dimension_numbers=('NWC', 'WIO', 'NWC'),
                                   feature_group_count=D_XBC) + conv_b
    xbc = jax.nn.silu(xbc.astype(jnp.float32))
    xs, Bm, Cm = jnp.split(xbc, [D_SSM, D_SSM + SSM_GROUPS * SSM_STATE], axis=-1)
    rep = SSM_HEADS // SSM_GROUPS
    xs = xs.reshape(B, S, SSM_HEADS, SSM_HEAD_DIM)
    Bm = jnp.repeat(Bm.reshape(B, S, SSM_GROUPS, SSM_STATE), rep, axis=2)
    Cm = jnp.repeat(Cm.reshape(B, S, SSM_GROUPS, SSM_STATE), rep, axis=2)
    dt = jax.nn.softplus(dt_raw.astype(jnp.float32).reshape(B, S, 2, SSM_HEADS) + dt_bias.astype(jnp.float32))
    A = -jnp.exp(a_log.astype(jnp.float32))
    flip = lambda t: jnp.flip(t, axis=1)
    y_f = ssd_scan(xs, dt[:, :, 0], A[0], Bm, Cm)
    y_b = flip(ssd_scan(flip(xs), flip(dt[:, :, 1]), A[1], flip(Bm), flip(Cm)))
    y = y_f + y_b + d_skip.astype(jnp.float32)[:, None] * xs
    y = y.reshape(B, S, D_SSM) * jax.nn.silu(z.astype(jnp.float32))
    yg = y.reshape(B, S, SSM_GROUPS, D_SSM // SSM_GROUPS)
    yg = yg * lax.rsqrt(jnp.mean(yg * yg, axis=-1, keepdims=True) + EPS)
    y = yg.reshape(B, S, D_SSM) * norm_g.astype(jnp.float32)
    return y.astype(z.dtype)


def hybrid_mixer(hn, w_in, q_norm_g, k_norm_g, attn_out_g, conv_w, conv_b, a_log, dt_bias,
                 d_skip, ssm_norm_g, w_out):
    proj = hn @ w_in
    cuts = [D_ATTN, D_ATTN + D_KV, D_ATTN + 2 * D_KV, D_ATTN + 2 * D_KV + D_SSM,
            D_ATTN + 2 * D_KV + D_SSM + D_XBC]
    q, k, v, z, xbc, dt_raw = jnp.split(proj, cuts, axis=-1)
    attn = rmsnorm(grid_attention(q, k, v, q_norm_g, k_norm_g), attn_out_g)
    ssm = ssd_mixer(z, xbc, dt_raw, conv_w, conv_b, a_log, dt_bias, d_skip, ssm_norm_g)
    return jnp.concatenate([attn, ssm], axis=-1) @ w_out


def hier_moe(h, w_rg, b_rg, w_re, b_re, w1, w3, w2):
    T, D = h.shape
    hf = h.astype(jnp.float32)
    g_logits = hf @ w_rg.astype(jnp.float32) + b_rg.astype(jnp.float32)
    g_prob = jax.nn.softmax(g_logits, axis=-1)
    g_idx = jnp.argmax(g_logits, axis=-1)
    g_w = jnp.take_along_axis(g_prob, g_idx[:, None], axis=-1)[:, 0]
    e_all = jnp.einsum('td,gde->tge', hf, w_re.astype(jnp.float32)) + b_re.astype(jnp.float32)
    e_logits = jnp.take_along_axis(e_all, g_idx[:, None, None], axis=1)[:, 0]
    top_v, top_i = lax.top_k(e_logits, TOP_K)
    e_w = jax.nn.softmax(top_v, axis=-1)
    wts = (g_w[:, None] * e_w).reshape(-1)
    eid = (g_idx[:, None] * EXPERTS_PER_GROUP + top_i).reshape(-1)
    tok = jnp.repeat(jnp.arange(T), TOP_K)
    n_assign = T * TOP_K
    order = jnp.argsort(eid)
    s_eid, s_tok, s_w = eid[order], tok[order], wts[order]
    counts = jnp.bincount(eid, length=N_EXPERTS)
    starts = jnp.cumsum(counts) - counts
    padded = (counts + MOE_BLOCK - 1) // MOE_BLOCK * MOE_BLOCK
    pad_ends = jnp.cumsum(padded)
    pad_starts = pad_ends - padded
    dest = pad_starts[s_eid] + (jnp.arange(n_assign) - starts[s_eid])
    n_blocks = (n_assign + MOE_BLOCK - 1) // MOE_BLOCK + N_EXPERTS
    R = n_blocks * MOE_BLOCK
    row_tok = jnp.zeros((R,), jnp.int32).at[dest].set(s_tok.astype(jnp.int32))
    row_w = jnp.zeros((R,), jnp.float32).at[dest].set(s_w)
    blk_eid = jnp.minimum(jnp.searchsorted(pad_ends, jnp.arange(n_blocks) * MOE_BLOCK, side='right'),
                          N_EXPERTS - 1)
    xin = h[row_tok].reshape(n_blocks, MOE_BLOCK, D)

    def expert_block(args):
        xb, e = args
        return (jax.nn.silu(xb @ w1[e]) * (xb @ w3[e])) @ w2[e]

    yb = lax.map(expert_block, (xin, blk_eid)).reshape(R, D)
    out = jnp.zeros((T, D), jnp.float32).at[row_tok].add(row_w[:, None] * yb.astype(jnp.float32))
    return out.astype(h.dtype)


def trunk(x, c, w_ada, b_ada, norm1_g, w_in, q_norm_g, k_norm_g, attn_out_g, conv_w, conv_b,
          a_log, dt_bias, d_skip, ssm_norm_g, w_out, norm2_g, w_rg, b_rg, w_re, b_re, w1, w3, w2,
          final_g):
    B, S, D = x.shape
    h = x
    for l in range(DEPTH):
        mod = (jax.nn.silu(c) @ w_ada[l] + b_ada[l]).reshape(B, N_MOD, 1, D)
        shift1, scale1, gate1 = mod[:, 0], mod[:, 1], mod[:, 2]
        shift2, scale2, gate2 = mod[:, 3], mod[:, 4], mod[:, 5]
        n1 = rmsnorm(h, norm1_g[l]) * (1 + scale1) + shift1
        mix = hybrid_mixer(n1, w_in[l], q_norm_g[l], k_norm_g[l], attn_out_g[l], conv_w[l], conv_b[l],
                           a_log[l], dt_bias[l], d_skip[l], ssm_norm_g[l], w_out[l])
        h = h + gate1 * mix
        n2 = rmsnorm(h, norm2_g[l]) * (1 + scale2) + shift2
        ffn = hier_moe(n2.reshape(B * S, D), w_rg[l], b_rg[l], w_re[l], b_re[l], w1[l], w3[l], w2[l])
        h = h + gate2 * ffn.reshape(B, S, D)
    return rmsnorm(h, final_g)


def setup_inputs(seed: int = 0) -> dict:
    key = jax.random.key(seed)
    ks = jax.random.split(key, 32)
    f32 = jnp.float32
    nrm = lambda k, shape, s: jax.random.normal(k, shape, f32) * s
    gain = lambda k, shape: 1.0 + 0.02 * jax.random.normal(k, shape, f32)
    dt0 = jnp.exp(jax.random.uniform(ks[12], (DEPTH, 2, SSM_HEADS), f32, np.log(1e-3), np.log(1e-1)))
    return {
        "x_prompt": nrm(ks[0], (BATCH, SEQ, D_MODEL), 1.0),
        "x_sample": nrm(ks[1], (DEC_BATCH, DEC_SEQ, D_MODEL), 1.0),
        "c_prompt": nrm(ks[2], (BATCH, D_MODEL), 1.0),
        "c_sample": nrm(ks[3], (DEC_BATCH, D_MODEL), 1.0),
        "w_ada": nrm(ks[4], (DEPTH, D_MODEL, N_MOD * D_MODEL), 0.5 * D_MODEL ** -0.5),
        "b_ada": nrm(ks[5], (DEPTH, N_MOD * D_MODEL), 0.02),
        "norm1_g": gain(ks[6], (DEPTH, D_MODEL)),
        "w_in": nrm(ks[7], (DEPTH, D_MODEL, D_IN_PROJ), D_MODEL ** -0.5),
        "q_norm_g": gain(ks[8], (DEPTH, HEAD_DIM)),
        "k_norm_g": gain(ks[9], (DEPTH, HEAD_DIM)),
        "attn_out_g": gain(ks[10], (DEPTH, D_ATTN)),
        "conv_w": nrm(ks[11], (DEPTH, CONV_K, 1, D_XBC), CONV_K ** -0.5),
        "conv_b": nrm(ks[13], (DEPTH, D_XBC), 0.02),
        "a_log": jnp.log(jax.random.uniform(ks[14], (DEPTH, 2, SSM_HEADS), f32, 1.0, 16.0)),
        "dt_bias": dt0 + jnp.log(-jnp.expm1(-dt0)),
        "d_skip": 1.0 + 0.1 * jax.random.normal(ks[15], (DEPTH, SSM_HEADS), f32),
        "ssm_norm_g": gain(ks[16], (DEPTH, D_SSM)),
        "w_out": nrm(ks[17], (DEPTH, D_MIX, D_MODEL), D_MIX ** -0.5),
        "norm2_g": gain(ks[18], (DEPTH, D_MODEL)),
        "w_rg": nrm(ks[19], (DEPTH, D_MODEL, N_EGROUPS), D_MODEL ** -0.5),
        "b_rg": nrm(ks[20], (DEPTH, N_EGROUPS), 0.01),
        "w_re": nrm(ks[21], (DEPTH, N_EGROUPS, D_MODEL, EXPERTS_PER_GROUP), D_MODEL ** -0.5),
        "b_re": nrm(ks[22], (DEPTH, N_EGROUPS, EXPERTS_PER_GROUP), 0.01),
        "w1": nrm(ks[23], (DEPTH, N_EXPERTS, D_MODEL, D_EXPERT), D_MODEL ** -0.5),
        "w3": nrm(ks[24], (DEPTH, N_EXPERTS, D_MODEL, D_EXPERT), D_MODEL ** -0.5),
        "w2": nrm(ks[25], (DEPTH, N_EXPERTS, D_EXPERT, D_MODEL), D_EXPERT ** -0.5),
        "final_g": gain(ks[26], (D_MODEL,)),
    }


def reference(x_prompt, x_sample, c_prompt, c_sample, w_ada, b_ada, norm1_g, w_in, q_norm_g,
              k_norm_g, attn_out_g, conv_w, conv_b, a_log, dt_bias, d_skip, ssm_norm_g, w_out,
              norm2_g, w_rg, b_rg, w_re, b_re, w1, w3, w2, final_g):
    y_prompt = trunk(x_prompt, c_prompt, w_ada, b_ada, norm1_g, w_in, q_norm_g, k_norm_g, attn_out_g,
                     conv_w, conv_b, a_log, dt_bias, d_skip, ssm_norm_g, w_out, norm2_g, w_rg, b_rg,
                     w_re, b_re, w1, w3, w2, final_g)
    y_sample = trunk(x_sample, c_sample, w_ada, b_ada, norm1_g, w_in, q_norm_g, k_norm_g, attn_out_g,
                     conv_w, conv_b, a_log, dt_bias, d_skip, ssm_norm_g, w_out, norm2_g, w_rg, b_rg,
                     w_re, b_re, w1, w3, w2, final_g)
    return (y_prompt, y_sample)
```

```python
import functools

import jax
import jax.numpy as jnp
from jax import lax
from jax.experimental import pallas as pl
from jax.experimental.pallas import tpu as pltpu

F32 = jnp.float32
BF16 = jnp.bfloat16
HIGHEST = lax.Precision.HIGHEST

D_MODEL = 1024
GRID_W = 64
EPS = 1e-6
N_HEADS = 8
N_KV_HEADS = 2
HEAD_DIM = 64
D_ATTN = N_HEADS * HEAD_DIM
D_KV = N_KV_HEADS * HEAD_DIM
ROPE_THETA = 10000.0
SSM_HEADS = 8
SSM_HEAD_DIM = 64
D_SSM = SSM_HEADS * SSM_HEAD_DIM
SSM_STATE = 64
SSM_GROUPS = 2
CONV_K = 5
CHUNK = 128
D_XBC = D_SSM + 2 * SSM_GROUPS * SSM_STATE
D_IN_PROJ = D_ATTN + 2 * D_KV + 2 * D_SSM + 2 * SSM_GROUPS * SSM_STATE + 2 * SSM_HEADS
N_EGROUPS = 4
EXPERTS_PER_GROUP = 8
N_EXPERTS = N_EGROUPS * EXPERTS_PER_GROUP
D_EXPERT = 512
N_MOD = 6

LANES = 128
SUBLANES = 8
ROW_TILES = D_MODEL // LANES
D_IN_PAD = (D_IN_PROJ + LANES - 1) // LANES * LANES
SMEM_IDX = 1024
TOK_TILE = SMEM_IDX // 2
MOE_ROWS = 256
VMEM_LIMIT = 48 * 1024 * 1024

_Q0, _K0, _V0, _Z0, _X0, _DT0 = 0, D_ATTN, D_ATTN + D_KV, D_ATTN + 2 * D_KV, D_ATTN + 2 * D_KV + D_SSM, \
    D_ATTN + 2 * D_KV + D_SSM + D_XBC


def _params(sem):
    return pltpu.CompilerParams(dimension_semantics=sem, vmem_limit_bytes=VMEM_LIMIT)


def _silu(x):
    return x * jax.nn.sigmoid(x)


def _mod_kernel(c_ref, w_ref, b_ref, o_ref):
    o_ref[...] = jnp.dot(_silu(c_ref[...]), w_ref[...], precision=HIGHEST,
                         preferred_element_type=F32) + b_ref[...]


def _modulation(c, w, b):
    rows, n = c.shape[0], w.shape[1]
    tn = 1024
    return pl.pallas_call(
        _mod_kernel,
        grid=(n // tn,),
        in_specs=[pl.BlockSpec((rows, D_MODEL), lambda j: (0, 0)),
                  pl.BlockSpec((D_MODEL, tn), lambda j: (0, j)),
                  pl.BlockSpec((1, tn), lambda j: (0, j))],
        out_specs=pl.BlockSpec((rows, tn), lambda j: (0, j)),
        out_shape=jax.ShapeDtypeStruct((rows, n), F32),
        compiler_params=_params(("arbitrary",)),
        name="mod",
    )(c, w, b)


def _inproj_kernel(x_ref, mod_ref, g1_ref, w_ref, qg_ref, kg_ref, cos_ref, sin_ref, seg_ref,
                   q_ref, k_ref, v_ref, z_ref, xbc_ref, dt_ref):
    x = x_ref[0]
    ms = jnp.mean(x * x, axis=-1, keepdims=True)
    xn = x * lax.rsqrt(ms + EPS) * g1_ref[...]
    n1 = xn * (1.0 + mod_ref[0, 1:2, :]) + mod_ref[0, 0:1, :]
    proj = jnp.dot(n1.astype(BF16), w_ref[...], preferred_element_type=F32)

    cos = cos_ref[...]
    sin = sin_ref[...]
    seg = seg_ref[...]
    lane = lax.broadcasted_iota(jnp.int32, cos.shape, 1)
    low_half = (lane % 32) < 16

    def norm_rope(xc, gain):
        sq = xc * xc
        hi = sq.astype(BF16)
        lo = (sq - hi.astype(F32)).astype(BF16)
        ss = (jnp.dot(hi, seg, preferred_element_type=F32)
              + jnp.dot(lo, seg, preferred_element_type=F32))
        xg = xc * lax.rsqrt(ss * (1.0 / HEAD_DIM) + EPS) * gain
        rot = jnp.where(low_half, pltpu.roll(xg, LANES - 16, 1), pltpu.roll(xg, 16, 1))
        return xg * cos + rot * sin

    for c in range(D_ATTN // LANES):
        sl = slice(c * LANES, (c + 1) * LANES)
        q_ref[0, :, sl] = norm_rope(proj[:, _Q0 + c * LANES:_Q0 + (c + 1) * LANES], qg_ref[:, sl]).astype(BF16)
    k_ref[0] = norm_rope(proj[:, _K0:_V0], kg_ref[...]).astype(BF16)
    v_ref[0] = proj[:, _V0:_Z0].astype(BF16)
    z_ref[0] = proj[:, _Z0:_X0]
    xbc_ref[0] = proj[:, _X0:_DT0]
    dt_ref[0] = proj[:, _DT0:D_IN_PAD]


def _in_proj(x, mod, g1, w_in, qg, kg, cos, sin, seg, tm):
    B, S, _ = x.shape
    row = lambda b, i: (b, i, 0)
    const = lambda b, i: (0, 0)
    outs = [(D_ATTN, BF16), (D_KV, BF16), (D_KV, BF16), (D_SSM, F32), (D_XBC, F32), (LANES, F32)]
    return pl.pallas_call(
        _inproj_kernel,
        grid=(B, S // tm),
        in_specs=[pl.BlockSpec((1, tm, D_MODEL), row),
                  pl.BlockSpec((1, N_MOD, D_MODEL), lambda b, i: (b, 0, 0)),
                  pl.BlockSpec((1, D_MODEL), const),
                  pl.BlockSpec((D_MODEL, D_IN_PAD), const),
                  pl.BlockSpec((1, D_ATTN), const),
                  pl.BlockSpec((1, D_KV), const),
                  pl.BlockSpec((tm, LANES), lambda b, i: (i, 0)),
                  pl.BlockSpec((tm, LANES), lambda b, i: (i, 0)),
                  pl.BlockSpec((LANES, LANES), const)],
        out_specs=[pl.BlockSpec((1, tm, n), row) for n, _ in outs],
        out_shape=[jax.ShapeDtypeStruct((B, S, n), dt) for n, dt in outs],
        compiler_params=_params(("parallel", "parallel")),
        name="in_proj",
    )(x, mod, g1, w_in, qg, kg, cos, sin, seg)


def _attn_kernel(q_ref, k_ref, v_ref, g_ref, o_ref, qs_ref, m_ref, l_ref, acc_ref, *, tq, tk, nk):
    G = N_HEADS // N_KV_HEADS
    lane = lax.broadcasted_iota(jnp.int32, (tq, LANES), 1)
    qf = q_ref[0].astype(F32)
    for j in range(N_KV_HEADS):
        pieces = []
        for g in range(G):
            h = G * j + g
            chunk = qf[:, (h // 2) * LANES:(h // 2 + 1) * LANES]
            if h % 2 != j:
                chunk = pltpu.roll(chunk, HEAD_DIM, 1)
            keep = (lane >= HEAD_DIM * j) & (lane < HEAD_DIM * (j + 1))
            pieces.append(jnp.where(keep, chunk, 0.0))
        qs_ref[j] = jnp.concatenate(pieces, axis=0).astype(BF16)
    m_ref[...] = jnp.full(m_ref.shape, -jnp.inf, F32)
    l_ref[...] = jnp.zeros(l_ref.shape, F32)
    acc_ref[...] = jnp.zeros(acc_ref.shape, F32)

    def body(c, carry):
        off = pl.multiple_of(c * tk, tk)
        kc = k_ref[0, pl.ds(off, tk), :]
        vc = v_ref[0, pl.ds(off, tk), :]
        for j in range(N_KV_HEADS):
            s = lax.dot_general(qs_ref[j], kc, (((1,), (1,)), ((), ())), preferred_element_type=F32)
            m_prev = m_ref[j]
            m_new = jnp.maximum(m_prev, jnp.max(s, axis=1, keepdims=True))
            alpha = jnp.exp(m_prev - m_new)
            p = jnp.exp(s - m_new)
            l_ref[j] = alpha * l_ref[j] + jnp.sum(p, axis=1, keepdims=True)
            acc_ref[j] = alpha * acc_ref[j] + jnp.dot(p.astype(BF16), vc, preferred_element_type=F32)
            m_ref[j] = m_new
        return carry

    lax.fori_loop(0, nk, body, 0)

    o = [acc_ref[j] / l_ref[j] for j in range(N_KV_HEADS)]

    def head(h, half):
        j, g = divmod(h, G)
        piece = o[j][g * tq:(g + 1) * tq, :]
        return pltpu.roll(piece, HEAD_DIM, 1) if j != half else piece

    of = jnp.concatenate(
        [jnp.where(lane < HEAD_DIM, head(2 * c, 0), head(2 * c + 1, 1)) for c in range(N_HEADS // 2)], axis=1)
    ms = jnp.mean(of * of, axis=-1, keepdims=True)
    o_ref[0] = (of * lax.rsqrt(ms + EPS) * g_ref[...]).astype(BF16)


def _attention(q, k, v, g, tq, tk):
    B, S, _ = q.shape
    G = N_HEADS // N_KV_HEADS
    kern = functools.partial(_attn_kernel, tq=tq, tk=tk, nk=S // tk)
    return pl.pallas_call(
        kern,
        grid=(B, S // tq),
        in_specs=[pl.BlockSpec((1, tq, D_ATTN), lambda b, i: (b, i, 0)),
                  pl.BlockSpec((1, S, D_KV), lambda b, i: (b, 0, 0)),
                  pl.BlockSpec((1, S, D_KV), lambda b, i: (b, 0, 0)),
                  pl.BlockSpec((1, D_ATTN), lambda b, i: (0, 0))],
        out_specs=pl.BlockSpec((1, tq, D_ATTN), lambda b, i: (b, i, 0)),
        out_shape=jax.ShapeDtypeStruct((B, S, D_ATTN), BF16),
        scratch_shapes=[pltpu.VMEM((N_KV_HEADS, G * tq, LANES), BF16),
                        pltpu.VMEM((N_KV_HEADS, G * tq, 1), F32),
                        pltpu.VMEM((N_KV_HEADS, G * tq, 1), F32),
                        pltpu.VMEM((N_KV_HEADS, G * tq, LANES), F32)],
        compiler_params=_params(("parallel", "parallel")),
        name="attn",
    )(q, k, v, g)


def _conv_kernel(xp_ref, xc_ref, xn_ref, w_ref, b_ref, o_ref, ext_ref, *, tr, nblk):
    i = pl.program_id(1)
    pad = CONV_K // 2
    ext_ref[0:SUBLANES, :] = jnp.where(i > 0, xp_ref[0], 0.0)
    ext_ref[SUBLANES:SUBLANES + tr, :] = xc_ref[0]
    ext_ref[SUBLANES + tr:, :] = jnp.where(i < nblk - 1, xn_ref[0], 0.0)
    acc = jnp.zeros((tr, D_XBC), F32) + b_ref[...]
    for kk in range(CONV_K):
        acc = acc + w_ref[kk:kk + 1, :] * ext_ref[SUBLANES - pad + kk:SUBLANES - pad + kk + tr, :]
    o_ref[0] = _silu(acc)


def _conv(xbc, w, b, tr):
    B, S, _ = xbc.shape
    nblk = S // tr
    per = tr // SUBLANES
    kern = functools.partial(_conv_kernel, tr=tr, nblk=nblk)
    return pl.pallas_call(
        kern,
        grid=(B, nblk),
        in_specs=[pl.BlockSpec((1, SUBLANES, D_XBC), lambda b, i: (b, jnp.maximum(i * per - 1, 0), 0)),
                  pl.BlockSpec((1, tr, D_XBC), lambda b, i: (b, i, 0)),
                  pl.BlockSpec((1, SUBLANES, D_XBC),
                               lambda b, i: (b, jnp.minimum((i + 1) * per, S // SUBLANES - 1), 0)),
                  pl.BlockSpec((CONV_K, D_XBC), lambda b, i: (0, 0)),
                  pl.BlockSpec((1, D_XBC), lambda b, i: (0, 0))],
        out_specs=pl.BlockSpec((1, tr, D_XBC), lambda b, i: (b, i, 0)),
        out_shape=jax.ShapeDtypeStruct((B, S, D_XBC), F32),
        scratch_shapes=[pltpu.VMEM((tr + 2 * SUBLANES, D_XBC), F32)],
        compiler_params=_params(("parallel", "parallel")),
        name="conv",
    )(xbc, xbc, xbc, w, b)


def _ssd_kernel(*refs, rev, rows):
    if rev:
        xbc_ref, dt_ref, dtb_ref, alog_ref, o_ref, st_ref = refs
    else:
        xbc_ref, dt_ref, dtb_ref, alog_ref, yb_ref, z_ref, dsk_ref, ng_ref, o_ref, st_ref = refs
    L = CHUNK
    d = 1 if rev else 0
    edge = 0 if rev else L - 1
    nchunk = rows // L

    @pl.when(pl.program_id(1) == 0)
    def _():
        st_ref[...] = jnp.zeros(st_ref.shape, F32)

    r_i = lax.broadcasted_iota(jnp.int32, (L, L), 0)
    c_i = lax.broadcasted_iota(jnp.int32, (L, L), 1)
    causal = (c_i >= r_i) if rev else (r_i >= c_i)
    tri = causal.astype(F32)
    low = c_i < SSM_HEAD_DIM
    a_row = -jnp.exp(alog_ref[...])

    def chunk(ci, carry):
        cc = (nchunk - 1 - ci) if rev else ci
        off = pl.multiple_of(cc * L, L)
        xbc = xbc_ref[0, pl.ds(off, L), :]
        xs = xbc[:, :D_SSM]
        bm = xbc[:, D_SSM:D_SSM + LANES]
        cm = xbc[:, D_SSM + LANES:]
        x_dt = dt_ref[0, pl.ds(off, L), :] + dtb_ref[...]
        dt_all = jnp.maximum(x_dt, 0.0) + jnp.log(1.0 + jnp.exp(-jnp.abs(x_dt)))
        a_all = dt_all * a_row
        acs = jnp.dot(tri, a_all, precision=HIGHEST, preferred_element_type=F32)
        acs_t = acs.T
        bm_b = bm.astype(BF16)
        bm_t = bm.T.astype(BF16)
        cg = [jnp.where(low, cm, 0.0).astype(BF16), jnp.where(low, 0.0, cm).astype(BF16)]
        cb = [lax.dot_general(cg[g], bm_b, (((1,), (1,)), ((), ())), preferred_element_type=F32)
              for g in range(SSM_GROUPS)]
        ys = []
        for c in range(SSM_HEADS // 2):
            g = (2 * c) // (SSM_HEADS // SSM_GROUPS)
            ja, jb = d * SSM_HEADS + 2 * c, d * SSM_HEADS + 2 * c + 1
            sl = slice(c * LANES, (c + 1) * LANES)
            dt_pair = jnp.where(low, dt_all[:, ja:ja + 1], dt_all[:, jb:jb + 1])
            acs_pair = jnp.where(low, acs[:, ja:ja + 1], acs[:, jb:jb + 1])
            xdt = xs[:, sl] * dt_pair
            xdt_b = xdt.astype(BF16)
            y_heads = []
            for jh in (ja, jb):
                seg = acs[:, jh:jh + 1] - acs_t[jh:jh + 1, :]
                lmat = jnp.exp(jnp.where(causal, seg, -jnp.inf))
                y_heads.append(jnp.dot((cb[g] * lmat).astype(BF16), xdt_b, preferred_element_type=F32))
            y = jnp.where(low, y_heads[0], y_heads[1])
            st = st_ref[:, sl]
            y = y + jnp.dot(cg[g], st.astype(BF16), preferred_element_type=F32) * jnp.exp(acs_pair)
            ys.append(y)
            edge_row = acs_pair[edge:edge + 1, :]
            xdec = (xdt * jnp.exp(edge_row - acs_pair)).astype(BF16)
            st_ref[:, sl] = jnp.exp(edge_row) * st + jnp.dot(bm_t, xdec, preferred_element_type=F32)
        y = jnp.concatenate(ys, axis=1)
        if rev:
            o_ref[0, pl.ds(off, L), :] = y
        else:
            y = y + yb_ref[0, pl.ds(off, L), :] + dsk_ref[...] * xs
            y = y * _silu(z_ref[0, pl.ds(off, L), :])
            gw = D_SSM // SSM_GROUPS
            outs = []
            for g in range(SSM_GROUPS):
                yg = y[:, g * gw:(g + 1) * gw]
                outs.append(yg * lax.rsqrt(jnp.mean(yg * yg, axis=-1, keepdims=True) + EPS))
            o_ref[0, pl.ds(off, L), :] = (jnp.concatenate(outs, axis=1) * ng_ref[...]).astype(BF16)
        return carry

    lax.fori_loop(0, nchunk, chunk, 0)


def _ssd(xbc, dt, dtb, alog, extra, rev, rows):
    B, S, _ = xbc.shape
    nblk = S // rows
    blk = (lambda b, i: (b, nblk - 1 - i, 0)) if rev else (lambda b, i: (b, i, 0))
    const = lambda b, i: (0, 0)
    in_specs = [pl.BlockSpec((1, rows, D_XBC), blk),
                pl.BlockSpec((1, rows, LANES), blk),
                pl.BlockSpec((1, LANES), const),
                pl.BlockSpec((1, LANES), const)]
    args = [xbc, dt, dtb, alog]
    if not rev:
        yb, z, dsk, ng = extra
        in_specs += [pl.BlockSpec((1, rows, D_SSM), blk), pl.BlockSpec((1, rows, D_SSM), blk),
                     pl.BlockSpec((1, D_SSM), const), pl.BlockSpec((1, D_SSM), const)]
        args += [yb, z, dsk, ng]
    return pl.pallas_call(
        functools.partial(_ssd_kernel, rev=rev, rows=rows),
        grid=(B, nblk),
        in_specs=in_specs,
        out_specs=pl.BlockSpec((1, rows, D_SSM), blk),
        out_shape=jax.ShapeDtypeStruct((B, S, D_SSM), F32 if rev else BF16),
        scratch_shapes=[pltpu.VMEM((SSM_GROUPS * SSM_STATE, D_SSM), F32)],
        compiler_params=_params(("parallel", "arbitrary")),
        name="ssd_bwd" if rev else "ssd_fwd",
    )(*args)


def _outproj_kernel(a_ref, s_ref, x_ref, mod_ref, wa_ref, ws_ref, g2_ref, wr_ref, br_ref,
                    h_ref, n2_ref, eid_ref, wts_ref):
    mix = (jnp.dot(a_ref[0], wa_ref[...], preferred_element_type=F32)
           + jnp.dot(s_ref[0], ws_ref[...], preferred_element_type=F32))
    h = x_ref[0] + mod_ref[0, 2:3, :] * mix
    h_ref[0] = h
    hn = h * lax.rsqrt(jnp.mean(h * h, axis=-1, keepdims=True) + EPS) * g2_ref[...]
    n2 = hn * (1.0 + mod_ref[0, 4:5, :]) + mod_ref[0, 3:4, :]
    n2_ref[0] = n2

    logits = jnp.dot(n2, wr_ref[...], precision=HIGHEST, preferred_element_type=F32) + br_ref[...]
    lane = lax.broadcasted_iota(jnp.int32, logits.shape, 1)
    lane_f = lane.astype(F32)
    big = float(LANES)
    is_g = lane < N_EGROUPS
    gl = jnp.where(is_g, logits, -jnp.inf)
    gmax = jnp.max(gl, axis=1, keepdims=True)
    gidx = jnp.min(jnp.where(gl == gmax, lane_f, big), axis=1, keepdims=True)
    g_w = 1.0 / jnp.sum(jnp.where(is_g, jnp.exp(gl - gmax), 0.0), axis=1, keepdims=True)
    e_lo = N_EGROUPS + EXPERTS_PER_GROUP * gidx
    sel = (lane_f >= e_lo) & (lane_f < e_lo + EXPERTS_PER_GROUP)
    el = jnp.where(sel, logits, -jnp.inf)
    v1 = jnp.max(el, axis=1, keepdims=True)
    i1 = jnp.min(jnp.where(el == v1, lane_f, big), axis=1, keepdims=True)
    el2 = jnp.where(lane_f == i1, -jnp.inf, el)
    v2 = jnp.max(el2, axis=1, keepdims=True)
    i2 = jnp.min(jnp.where(el2 == v2, lane_f, big), axis=1, keepdims=True)
    t = jnp.exp(v2 - v1)
    den = 1.0 + t
    w_a = g_w * (1.0 / den)
    w_b = g_w * (t / den)
    eid_ref[0] = jnp.where(lane == 0, i1 - N_EGROUPS, jnp.where(lane == 1, i2 - N_EGROUPS, 0.0)).astype(jnp.int32)
    wts_ref[0] = jnp.where(lane == 0, w_a, jnp.where(lane == 1, w_b, 0.0))


def _out_proj(attn, ssm, x, mod, wa, ws, g2, wr, br, tm):
    B, S, _ = x.shape
    row = lambda b, i: (b, i, 0)
    const = lambda b, i: (0, 0)
    return pl.pallas_call(
        _outproj_kernel,
        grid=(B, S // tm),
        in_specs=[pl.BlockSpec((1, tm, D_ATTN), row),
                  pl.BlockSpec((1, tm, D_SSM), row),
                  pl.BlockSpec((1, tm, D_MODEL), row),
                  pl.BlockSpec((1, N_MOD, D_MODEL), lambda b, i: (b, 0, 0)),
                  pl.BlockSpec((D_ATTN, D_MODEL), const),
                  pl.BlockSpec((D_SSM, D_MODEL), const),
                  pl.BlockSpec((1, D_MODEL), const),
                  pl.BlockSpec((D_MODEL, LANES), const),
                  pl.BlockSpec((1, LANES), const)],
        out_specs=[pl.BlockSpec((1, tm, D_MODEL), row), pl.BlockSpec((1, tm, D_MODEL), row),
                   pl.BlockSpec((1, tm, LANES), row), pl.BlockSpec((1, tm, LANES), row)],
        out_shape=[jax.ShapeDtypeStruct((B, S, D_MODEL), F32), jax.ShapeDtypeStruct((B, S, D_MODEL), F32),
                   jax.ShapeDtypeStruct((B, S, LANES), jnp.int32), jax.ShapeDtypeStruct((B, S, LANES), F32)],
        compiler_params=_params(("parallel", "parallel")),
        name="out_proj",
    )(attn, ssm, x, mod, wa, ws, g2, wr, br)


def _rank_kernel(eid_ref, rank_ref, cnt_ref, carry_ref, *, tt):
    @pl.when(pl.program_id(0) == 0)
    def _():
        carry_ref[...] = jnp.zeros(carry_ref.shape, F32)

    eid = eid_ref[...]
    lane = lax.broadcasted_iota(jnp.int32, eid.shape, 1)
    r_i = lax.broadcasted_iota(jnp.int32, (tt, tt), 0)
    c_i = lax.broadcasted_iota(jnp.int32, (tt, tt), 1)
    before = (c_i < r_i).astype(BF16)
    carry = carry_ref[0:1, :]
    ranks = []
    for k in range(2):
        oh = lane == eid[:, k:k + 1]
        oh_f = oh.astype(F32)
        pref = jnp.dot(before, oh.astype(BF16), preferred_element_type=F32) + carry
        ranks.append(jnp.sum(oh_f * pref, axis=1, keepdims=True))
        carry = carry + jnp.sum(oh_f, axis=0, keepdims=True)
    carry_ref[0:1, :] = carry
    rank_ref[...] = jnp.where(lane == 0, ranks[0], jnp.where(lane == 1, ranks[1], 0.0)).astype(jnp.int32)
    cnt_ref[...] = jnp.broadcast_to(carry, cnt_ref.shape).astype(jnp.int32)


def _rank(eid, tt):
    T = eid.shape[0]
    return pl.pallas_call(
        functools.partial(_rank_kernel, tt=tt),
        grid=(T // tt,),
        in_specs=[pl.BlockSpec((tt, LANES), lambda i: (i, 0))],
        out_specs=[pl.BlockSpec((tt, LANES), lambda i: (i, 0)),
                   pl.BlockSpec((SUBLANES, LANES), lambda i: (0, 0))],
        out_shape=[jax.ShapeDtypeStruct((T, LANES), jnp.int32),
                   jax.ShapeDtypeStruct((SUBLANES, LANES), jnp.int32)],
        scratch_shapes=[pltpu.VMEM((SUBLANES, LANES), F32)],
        compiler_params=_params(("arbitrary",)),
        name="rank",
    )(eid)


def _row_copy(src, dst, sem):
    return pltpu.make_async_copy(src, dst, sem)


def _dispatch_kernel(dest_hbm, n2_hbm, xin_in, xin_hbm, idx_ref, idx_sem, row_sem):
    del xin_in
    i = pl.program_id(0)
    idx_cp = pltpu.make_async_copy(dest_hbm.at[pl.ds(pl.multiple_of(i * SMEM_IDX, SMEM_IDX), SMEM_IDX)],
                                   idx_ref, idx_sem)
    idx_cp.start()
    idx_cp.wait()
    base = i * TOK_TILE

    def issue(t, carry):
        for k in range(2):
            _row_copy(n2_hbm.at[base + t], xin_hbm.at[idx_ref[2 * t + k]], row_sem).start()
        return carry

    lax.fori_loop(0, TOK_TILE, issue, 0)

    def drain(t, carry):
        for k in range(2):
            _row_copy(n2_hbm.at[0], xin_hbm.at[0], row_sem).wait()
        return carry

    lax.fori_loop(0, TOK_TILE, drain, 0)


def _dispatch(dest, n2_rows, xin_zero):
    T = n2_rows.shape[0]
    any_spec = pl.BlockSpec(memory_space=pl.ANY)
    return pl.pallas_call(
        _dispatch_kernel,
        grid=(T // TOK_TILE,),
        in_specs=[any_spec, any_spec, any_spec],
        out_specs=any_spec,
        out_shape=jax.ShapeDtypeStruct(xin_zero.shape, F32),
        scratch_shapes=[pltpu.SMEM((SMEM_IDX,), jnp.int32), pltpu.SemaphoreType.DMA, pltpu.SemaphoreType.DMA],
        input_output_aliases={2: 0},
        compiler_params=pltpu.CompilerParams(dimension_semantics=("arbitrary",), has_side_effects=True),
        name="dispatch",
    )(dest, n2_rows, xin_zero)


def _moe_kernel(beid_ref, nused_ref, xin_ref, w1_ref, w3_ref, w2_ref, y_ref):
    del beid_ref
    i = pl.program_id(0)

    @pl.when(i < nused_ref[0])
    def _():
        x = jnp.concatenate([xin_ref[:, s, :] for s in range(ROW_TILES)], axis=1).astype(BF16)
        h1 = jnp.dot(x, w1_ref[0], preferred_element_type=F32)
        h3 = jnp.dot(x, w3_ref[0], preferred_element_type=F32)
        y = jnp.dot((_silu(h1) * h3).astype(BF16), w2_ref[0], preferred_element_type=F32)
        for s in range(ROW_TILES):
            y_ref[:, s, :] = y[:, s * LANES:(s + 1) * LANES]

    @pl.when(i >= nused_ref[0])
    def _():
        y_ref[...] = jnp.zeros(y_ref.shape, F32)


def _moe(blk_eid, n_used, xin, w1, w3, w2):
    R = xin.shape[0]
    grid_spec = pltpu.PrefetchScalarGridSpec(
        num_scalar_prefetch=2,
        grid=(R // MOE_ROWS,),
        in_specs=[pl.BlockSpec((MOE_ROWS, ROW_TILES, LANES), lambda i, be, nu: (i, 0, 0)),
                  pl.BlockSpec((1, D_MODEL, D_EXPERT), lambda i, be, nu: (be[i], 0, 0)),
                  pl.BlockSpec((1, D_MODEL, D_EXPERT), lambda i, be, nu: (be[i], 0, 0)),
                  pl.BlockSpec((1, D_EXPERT, D_MODEL), lambda i, be, nu: (be[i], 0, 0))],
        out_specs=pl.BlockSpec((MOE_ROWS, ROW_TILES, LANES), lambda i, be, nu: (i, 0, 0)),
    )
    return pl.pallas_call(
        _moe_kernel,
        grid_spec=grid_spec,
        out_shape=jax.ShapeDtypeStruct(xin.shape, F32),
        compiler_params=_params(("arbitrary",)),
        name="moe",
    )(blk_eid, n_used, xin, w1, w3, w2)


def _combine_kernel(dest_hbm, y_hbm, h_ref, wts_ref, mod_ref, fg_ref, o_ref, idx_ref, buf_ref, idx_sem, row_sem,
                    *, tiles_per_seq):
    i = pl.program_id(0) * tiles_per_seq + pl.program_id(1)
    idx_cp = pltpu.make_async_copy(dest_hbm.at[pl.ds(pl.multiple_of(i * SMEM_IDX, SMEM_IDX), SMEM_IDX)],
                                   idx_ref, idx_sem)
    idx_cp.start()
    idx_cp.wait()

    def issue(t, carry):
        for k in range(2):
            _row_copy(y_hbm.at[idx_ref[2 * t + k]], buf_ref.at[k, t], row_sem).start()
        return carry

    lax.fori_loop(0, TOK_TILE, issue, 0)

    def drain(t, carry):
        for k in range(2):
            _row_copy(y_hbm.at[0], buf_ref.at[k, 0], row_sem).wait()
        return carry

    lax.fori_loop(0, TOK_TILE, drain, 0)

    wts = wts_ref[0]
    ffn = jnp.zeros((TOK_TILE, D_MODEL), F32)
    for k in range(2):
        yk = jnp.concatenate([buf_ref[k, :, s, :] for s in range(ROW_TILES)], axis=1)
        ffn = ffn + wts[:, k:k + 1] * yk
    h = h_ref[0] + mod_ref[0, 5:6, :] * ffn
    o_ref[0] = h * lax.rsqrt(jnp.mean(h * h, axis=-1, keepdims=True) + EPS) * fg_ref[...]


def _combine(dest, y_rows, h, wts, mod, fg):
    B, S, _ = h.shape
    tps = S // TOK_TILE
    row = lambda b, i: (b, i, 0)
    any_spec = pl.BlockSpec(memory_space=pl.ANY)
    return pl.pallas_call(
        functools.partial(_combine_kernel, tiles_per_seq=tps),
        grid=(B, tps),
        in_specs=[any_spec, any_spec,
                  pl.BlockSpec((1, TOK_TILE, D_MODEL), row),
                  pl.BlockSpec((1, TOK_TILE, LANES), row),
                  pl.BlockSpec((1, N_MOD, D_MODEL), lambda b, i: (b, 0, 0)),
                  pl.BlockSpec((1, D_MODEL), lambda b, i: (0, 0))],
        out_specs=pl.BlockSpec((1, TOK_TILE, D_MODEL), row),
        out_shape=jax.ShapeDtypeStruct((B, S, D_MODEL), F32),
        scratch_shapes=[pltpu.SMEM((SMEM_IDX,), jnp.int32),
                        pltpu.VMEM((2, TOK_TILE, ROW_TILES, LANES), F32),
                        pltpu.SemaphoreType.DMA, pltpu.SemaphoreType.DMA],
        compiler_params=_params(("arbitrary", "arbitrary")),
        name="combine",
    )(dest, y_rows, h, wts, mod, fg)


def _rope_tables(S):
    half = HEAD_DIM // 4
    inv = ROPE_THETA ** (-jnp.arange(half, dtype=F32) / half)
    t = jnp.arange(S)
    pos = jnp.stack([(t // GRID_W).astype(F32), (t % GRID_W).astype(F32)], axis=1)
    ang = pos[:, :, None] * inv[None, None, :]
    cos = jnp.cos(ang)
    sin = jnp.sin(ang)
    cos_h = jnp.concatenate([cos, cos], axis=-1).reshape(S, HEAD_DIM)
    sin_h = jnp.concatenate([-sin, sin], axis=-1).reshape(S, HEAD_DIM)
    reps = LANES // HEAD_DIM
    return jnp.tile(cos_h, (1, reps)), jnp.tile(sin_h, (1, reps))


def _moe_ffn(n2, h1, eid, wts, mod, w1, w3, w2, final_g):
    B, S, _ = n2.shape
    T = B * S
    n_blocks = (2 * T + MOE_ROWS - 1) // MOE_ROWS + N_EXPERTS
    R = n_blocks * MOE_ROWS
    eid2 = eid.reshape(T, LANES)
    rank, counts = _rank(eid2, 512)
    counts = counts[0, :N_EXPERTS]
    padded = (counts + MOE_ROWS - 1) // MOE_ROWS * MOE_ROWS
    pad_ends = jnp.cumsum(padded)
    pad_starts = pad_ends - padded
    e2 = eid2[:, :2]
    onehot = e2[:, :, None] == jnp.arange(N_EXPERTS, dtype=jnp.int32)[None, None, :]
    dest = (jnp.sum(jnp.where(onehot, pad_starts[None, None, :], 0), axis=-1) + rank[:, :2]).reshape(2 * T)
    blk_eid = jnp.minimum(
        jnp.searchsorted(pad_ends, jnp.arange(n_blocks, dtype=jnp.int32) * MOE_ROWS, side="right"),
        N_EXPERTS - 1).astype(jnp.int32)
    n_used = (pad_ends[-1:] // MOE_ROWS).astype(jnp.int32)
    xin = _dispatch(dest, n2.reshape(T, ROW_TILES, LANES), jnp.zeros((R, ROW_TILES, LANES), F32))
    y_rows = _moe(blk_eid, n_used, xin, w1, w3, w2)
    return _combine(dest, y_rows, h1, wts, mod, final_g)


def _trunk(x, mod, p):
    B, S, _ = x.shape
    cos, sin = _rope_tables(S)
    q, k, v, z, xbc, dtr = _in_proj(x, mod, p["g1"], p["w_in"], p["qg"], p["kg"], cos, sin, p["seg"], tm=512)
    attn = _attention(q, k, v, p["attn_g"], tq=128, tk=512)
    xact = _conv(xbc, p["conv_w"], p["conv_b"], tr=512)
    y_b = _ssd(xact, dtr, p["dt_bias"], p["a_log"], None, rev=True, rows=512)
    ssm = _ssd(xact, dtr, p["dt_bias"], p["a_log"], (y_b, z, p["d_skip"], p["ssm_g"]), rev=False, rows=512)
    h1, n2, eid, wts = _out_proj(attn, ssm, x, mod, p["w_out_a"], p["w_out_s"], p["g2"], p["w_r"], p["b_r"], tm=512)
    return _moe_ffn(n2, h1, eid, wts, mod, p["w1"], p["w3"], p["w2"], p["final_g"])


def kernel(x_prompt, x_sample, c_prompt, c_sample, w_ada, b_ada, norm1_g, w_in, q_norm_g, k_norm_g, attn_out_g,
           conv_w, conv_b, a_log, dt_bias, d_skip, ssm_norm_g, w_out, norm2_g, w_rg, b_rg, w_re, b_re, w1, w3, w2,
           final_g):
    l = 0
    Bp, Bs = c_prompt.shape[0], c_sample.shape[0]
    rows = (Bp + Bs + SUBLANES - 1) // SUBLANES * SUBLANES
    c_all = jnp.zeros((rows, D_MODEL), F32).at[:Bp].set(c_prompt).at[Bp:Bp + Bs].set(c_sample)
    mod = _modulation(c_all, w_ada[l], b_ada[l].reshape(1, -1))
    mod_p = mod[:Bp].reshape(Bp, N_MOD, D_MODEL)
    mod_s = mod[Bp:Bp + Bs].reshape(Bs, N_MOD, D_MODEL)

    def lane_pad(a):
        return jnp.pad(a.reshape(1, -1), ((0, 0), (0, LANES - a.size)))

    head_id = jnp.arange(LANES) // HEAD_DIM
    w_r = jnp.concatenate([w_rg[l]] + [w_re[l, g] for g in range(N_EGROUPS)], axis=1)
    b_r = jnp.concatenate([b_rg[l]] + [b_re[l, g] for g in range(N_EGROUPS)])
    p = {
        "g1": norm1_g[l].reshape(1, -1),
        "w_in": jnp.pad(w_in[l], ((0, 0), (0, D_IN_PAD - D_IN_PROJ))).astype(BF16),
        "qg": jnp.tile(q_norm_g[l] * (HEAD_DIM ** -0.5), N_HEADS).reshape(1, -1),
        "kg": jnp.tile(k_norm_g[l], N_KV_HEADS).reshape(1, -1),
        "seg": (head_id[:, None] == head_id[None, :]).astype(BF16),
        "attn_g": attn_out_g[l].reshape(1, -1),
        "conv_w": conv_w[l].reshape(CONV_K, D_XBC),
        "conv_b": conv_b[l].reshape(1, -1),
        "dt_bias": lane_pad(dt_bias[l]),
        "a_log": lane_pad(a_log[l]),
        "d_skip": jnp.repeat(d_skip[l], SSM_HEAD_DIM).reshape(1, -1),
        "ssm_g": ssm_norm_g[l].reshape(1, -1),
        "w_out_a": w_out[l, :D_ATTN].astype(BF16),
        "w_out_s": w_out[l, D_ATTN:].astype(BF16),
        "g2": norm2_g[l].reshape(1, -1),
        "w_r": jnp.pad(w_r, ((0, 0), (0, LANES - w_r.shape[1]))),
        "b_r": lane_pad(b_r),
        "w1": w1[l].astype(BF16),
        "w3": w3[l].astype(BF16),
        "w2": w2[l].astype(BF16),
        "final_g": final_g.reshape(1, -1),
    }
    return _trunk(x_prompt, mod_p, p), _trunk(x_sample, mod_s, p)
```

```python
import functools

import jax
import jax.numpy as jnp
from jax import lax
from jax.experimental import pallas as pl
from jax.experimental.pallas import tpu as pltpu

F32 = jnp.float32
BF16 = jnp.bfloat16
HIGHEST = lax.Precision.HIGHEST

D_MODEL = 1024
GRID_W = 64
EPS = 1e-6
N_HEADS = 8
N_KV_HEADS = 2
HEAD_DIM = 64
D_ATTN = N_HEADS * HEAD_DIM
D_KV = N_KV_HEADS * HEAD_DIM
ROPE_THETA = 10000.0
SSM_HEADS = 8
SSM_HEAD_DIM = 64
D_SSM = SSM_HEADS * SSM_HEAD_DIM
SSM_STATE = 64
SSM_GROUPS = 2
CONV_K = 5
CHUNK = 128
D_XBC = D_SSM + 2 * SSM_GROUPS * SSM_STATE
D_IN_PROJ = D_ATTN + 2 * D_KV + 2 * D_SSM + 2 * SSM_GROUPS * SSM_STATE + 2 * SSM_HEADS
N_EGROUPS = 4
EXPERTS_PER_GROUP = 8
N_EXPERTS = N_EGROUPS * EXPERTS_PER_GROUP
D_EXPERT = 512
N_MOD = 6

LANES = 128
SUBLANES = 8
ROW_TILES = D_MODEL // LANES
D_IN_PAD = (D_IN_PROJ + LANES - 1) // LANES * LANES
SMEM_IDX = 1024
TOK_TILE = SMEM_IDX // 2
MOE_ROWS = 256
VMEM_LIMIT = 48 * 1024 * 1024

_Q0, _K0, _V0, _Z0, _X0, _DT0 = 0, D_ATTN, D_ATTN + D_KV, D_ATTN + 2 * D_KV, D_ATTN + 2 * D_KV + D_SSM, \
    D_ATTN + 2 * D_KV + D_SSM + D_XBC


def _params(sem):
    return pltpu.CompilerParams(dimension_semantics=sem, vmem_limit_bytes=VMEM_LIMIT)


def _silu(x):
    return x * jax.nn.sigmoid(x)


def _mod_kernel(c_ref, w_ref, b_ref, o_ref):
    o_ref[...] = jnp.dot(_silu(c_ref[...]), w_ref[...], precision=HIGHEST,
                         preferred_element_type=F32) + b_ref[...]


def _modulation(c, w, b):
    rows, n = c.shape[0], w.shape[1]
    tn = 1024
    return pl.pallas_call(
        _mod_kernel,
        grid=(n // tn,),
        in_specs=[pl.BlockSpec((rows, D_MODEL), lambda j: (0, 0)),
                  pl.BlockSpec((D_MODEL, tn), lambda j: (0, j)),
                  pl.BlockSpec((1, tn), lambda j: (0, j))],
        out_specs=pl.BlockSpec((rows, tn), lambda j: (0, j)),
        out_shape=jax.ShapeDtypeStruct((rows, n), F32),
        compiler_params=_params(("arbitrary",)),
        name="mod",
    )(c, w, b)


def _inproj_kernel(x_ref, mod_ref, g1_ref, w_ref, qg_ref, kg_ref, cos_ref, sin_ref, seg_ref,
                   q_ref, k_ref, v_ref, z_ref, xbc_ref, dt_ref):
    x = x_ref[0]
    ms = jnp.mean(x * x, axis=-1, keepdims=True)
    xn = x * lax.rsqrt(ms + EPS) * g1_ref[...]
    n1 = xn * (1.0 + mod_ref[0, 1:2, :]) + mod_ref[0, 0:1, :]
    proj = jnp.dot(n1.astype(BF16), w_ref[...], preferred_element_type=F32)

    cos = cos_ref[...]
    sin = sin_ref[...]
    seg = seg_ref[...]
    lane = lax.broadcasted_iota(jnp.int32, cos.shape, 1)
    low_half = (lane % 32) < 16

    def norm_rope(xc, gain):
        sq = xc * xc
        hi = sq.astype(BF16)
        lo = (sq - hi.astype(F32)).astype(BF16)
        ss = (jnp.dot(hi, seg, preferred_element_type=F32)
              + jnp.dot(lo, seg, preferred_element_type=F32))
        xg = xc * lax.rsqrt(ss * (1.0 / HEAD_DIM) + EPS) * gain
        rot = jnp.where(low_half, pltpu.roll(xg, LANES - 16, 1), pltpu.roll(xg, 16, 1))
        return xg * cos + rot * sin

    for c in range(D_ATTN // LANES):
        sl = slice(c * LANES, (c + 1) * LANES)
        q_ref[0, :, sl] = norm_rope(proj[:, _Q0 + c * LANES:_Q0 + (c + 1) * LANES], qg_ref[:, sl]).astype(BF16)
    k_ref[0] = norm_rope(proj[:, _K0:_V0], kg_ref[...]).astype(BF16)
    v = proj[:, _V0:_Z0]
    for j in range(N_KV_HEADS):
        mine = (lane >= HEAD_DIM * j) & (lane < HEAD_DIM * (j + 1))
        v_ref[0, j] = jnp.where(mine, v, 1.0).astype(BF16)
    z_ref[0] = proj[:, _Z0:_X0]
    xbc_ref[0] = proj[:, _X0:_DT0]
    dt_ref[0] = proj[:, _DT0:D_IN_PAD]


def _in_proj(x, mod, g1, w_in, qg, kg, cos, sin, seg, tm):
    B, S, _ = x.shape
    row = lambda b, i: (b, i, 0)
    const = lambda b, i: (0, 0)
    outs = [(D_ATTN, BF16), (D_KV, BF16), None, (D_SSM, F32), (D_XBC, F32), (LANES, F32)]
    out_specs = [pl.BlockSpec((1, tm, o[0]), row) if o else
                 pl.BlockSpec((1, N_KV_HEADS, tm, D_KV), lambda b, i: (b, 0, i, 0)) for o in outs]
    out_shape = [jax.ShapeDtypeStruct((B, S, o[0]), o[1]) if o else
                 jax.ShapeDtypeStruct((B, N_KV_HEADS, S, D_KV), BF16) for o in outs]
    return pl.pallas_call(
        _inproj_kernel,
        grid=(B, S // tm),
        in_specs=[pl.BlockSpec((1, tm, D_MODEL), row),
                  pl.BlockSpec((1, N_MOD, D_MODEL), lambda b, i: (b, 0, 0)),
                  pl.BlockSpec((1, D_MODEL), const),
                  pl.BlockSpec((D_MODEL, D_IN_PAD), const),
                  pl.BlockSpec((1, D_ATTN), const),
                  pl.BlockSpec((1, D_KV), const),
                  pl.BlockSpec((tm, LANES), lambda b, i: (i, 0)),
                  pl.BlockSpec((tm, LANES), lambda b, i: (i, 0)),
                  pl.BlockSpec((LANES, LANES), const)],
        out_specs=out_specs,
        out_shape=out_shape,
        compiler_params=_params(("parallel", "parallel")),
        name="in_proj",
    )(x, mod, g1, w_in, qg, kg, cos, sin, seg)


def _attn_kernel(q_ref, k_ref, v_ref, g_ref, o_ref, qs_ref, m_ref, acc_ref, *, tq, tk, nk):
    G = N_HEADS // N_KV_HEADS
    lane = lax.broadcasted_iota(jnp.int32, (tq, LANES), 1)
    qf = q_ref[0].astype(F32)
    for j in range(N_KV_HEADS):
        pieces = []
        for g in range(G):
            h = G * j + g
            chunk = qf[:, (h // 2) * LANES:(h // 2 + 1) * LANES]
            if h % 2 != j:
                chunk = pltpu.roll(chunk, HEAD_DIM, 1)
            keep = (lane >= HEAD_DIM * j) & (lane < HEAD_DIM * (j + 1))
            pieces.append(jnp.where(keep, chunk, 0.0))
        qs_ref[j] = jnp.concatenate(pieces, axis=0).astype(BF16)
    m_ref[...] = jnp.full(m_ref.shape, -jnp.inf, F32)
    acc_ref[...] = jnp.zeros(acc_ref.shape, F32)

    def body(c, carry):
        off = pl.multiple_of(c * tk, tk)
        kc = k_ref[0, pl.ds(off, tk), :]
        for j in range(N_KV_HEADS):
            s = lax.dot_general(qs_ref[j], kc, (((1,), (1,)), ((), ())), preferred_element_type=F32)
            m_prev = m_ref[j]
            m_new = jnp.maximum(m_prev, jnp.max(s, axis=1, keepdims=True))
            alpha = jnp.exp(m_prev - m_new)
            p = jnp.exp(s - jnp.concatenate([m_new] * (tk // LANES), axis=1))
            acc_ref[j] = alpha * acc_ref[j] + jnp.dot(p.astype(BF16), v_ref[0, j, pl.ds(off, tk), :],
                                                      preferred_element_type=F32)
            m_ref[j] = m_new
        return carry

    lax.fori_loop(0, nk, body, 0, unroll=2)

    o = [acc_ref[j] / pltpu.roll(acc_ref[j], HEAD_DIM, 1) for j in range(N_KV_HEADS)]

    def head(h, half):
        j, g = divmod(h, G)
        piece = o[j][g * tq:(g + 1) * tq, :]
        return pltpu.roll(piece, HEAD_DIM, 1) if j != half else piece

    of = jnp.concatenate(
        [jnp.where(lane < HEAD_DIM, head(2 * c, 0), head(2 * c + 1, 1)) for c in range(N_HEADS // 2)], axis=1)
    ms = jnp.mean(of * of, axis=-1, keepdims=True)
    o_ref[0] = (of * lax.rsqrt(ms + EPS) * g_ref[...]).astype(BF16)


def _attention(q, k, v, g, tq, tk):
    B, S, _ = q.shape
    G = N_HEADS // N_KV_HEADS
    kern = functools.partial(_attn_kernel, tq=tq, tk=tk, nk=S // tk)
    return pl.pallas_call(
        kern,
        grid=(B, S // tq),
        in_specs=[pl.BlockSpec((1, tq, D_ATTN), lambda b, i: (b, i, 0)),
                  pl.BlockSpec((1, S, D_KV), lambda b, i: (b, 0, 0)),
                  pl.BlockSpec((1, N_KV_HEADS, S, D_KV), lambda b, i: (b, 0, 0, 0)),
                  pl.BlockSpec((1, D_ATTN), lambda b, i: (0, 0))],
        out_specs=pl.BlockSpec((1, tq, D_ATTN), lambda b, i: (b, i, 0)),
        out_shape=jax.ShapeDtypeStruct((B, S, D_ATTN), BF16),
        scratch_shapes=[pltpu.VMEM((N_KV_HEADS, G * tq, LANES), BF16),
                        pltpu.VMEM((N_KV_HEADS, G * tq, LANES), F32),
                        pltpu.VMEM((N_KV_HEADS, G * tq, LANES), F32)],
        compiler_params=_params(("parallel", "parallel")),
        name="attn",
    )(q, k, v, g)


def _conv_kernel(xp_ref, xc_ref, xn_ref, w_ref, b_ref, o_ref, ext_ref, *, tr, nblk):
    i = pl.program_id(1)
    pad = CONV_K // 2
    ext_ref[0:SUBLANES, :] = jnp.where(i > 0, xp_ref[0], 0.0)
    ext_ref[SUBLANES:SUBLANES + tr, :] = xc_ref[0]
    ext_ref[SUBLANES + tr:, :] = jnp.where(i < nblk - 1, xn_ref[0], 0.0)
    acc = jnp.zeros((tr, D_XBC), F32) + b_ref[...]
    for kk in range(CONV_K):
        acc = acc + w_ref[kk:kk + 1, :] * ext_ref[SUBLANES - pad + kk:SUBLANES - pad + kk + tr, :]
    o_ref[0] = _silu(acc)


def _conv(xbc, w, b, tr):
    B, S, _ = xbc.shape
    nblk = S // tr
    per = tr // SUBLANES
    kern = functools.partial(_conv_kernel, tr=tr, nblk=nblk)
    return pl.pallas_call(
        kern,
        grid=(B, nblk),
        in_specs=[pl.BlockSpec((1, SUBLANES, D_XBC), lambda b, i: (b, jnp.maximum(i * per - 1, 0), 0)),
                  pl.BlockSpec((1, tr, D_XBC), lambda b, i: (b, i, 0)),
                  pl.BlockSpec((1, SUBLANES, D_XBC),
                               lambda b, i: (b, jnp.minimum((i + 1) * per, S // SUBLANES - 1), 0)),
                  pl.BlockSpec((CONV_K, D_XBC), lambda b, i: (0, 0)),
                  pl.BlockSpec((1, D_XBC), lambda b, i: (0, 0))],
        out_specs=pl.BlockSpec((1, tr, D_XBC), lambda b, i: (b, i, 0)),
        out_shape=jax.ShapeDtypeStruct((B, S, D_XBC), F32),
        scratch_shapes=[pltpu.VMEM((tr + 2 * SUBLANES, D_XBC), F32)],
        compiler_params=_params(("parallel", "parallel")),
        name="conv",
    )(xbc, xbc, xbc, w, b)


def _ssd_kernel(*refs, rev, rows):
    if rev:
        xbc_ref, dt_ref, dtb_ref, alog_ref, o_ref, st_ref = refs
    else:
        xbc_ref, dt_ref, dtb_ref, alog_ref, yb_ref, z_ref, dsk_ref, ng_ref, o_ref, st_ref = refs
    L = CHUNK
    d = 1 if rev else 0
    edge = 0 if rev else L - 1
    nchunk = rows // L

    @pl.when(pl.program_id(1) == 0)
    def _():
        st_ref[...] = jnp.zeros(st_ref.shape, F32)

    r_i = lax.broadcasted_iota(jnp.int32, (L, L), 0)
    c_i = lax.broadcasted_iota(jnp.int32, (L, L), 1)
    causal = (c_i >= r_i) if rev else (r_i >= c_i)
    tri = causal.astype(F32)
    low = c_i < SSM_HEAD_DIM
    a_row = -jnp.exp(alog_ref[...])

    def chunk(ci, carry):
        cc = (nchunk - 1 - ci) if rev else ci
        off = pl.multiple_of(cc * L, L)
        xbc = xbc_ref[0, pl.ds(off, L), :]
        xs = xbc[:, :D_SSM]
        bm = xbc[:, D_SSM:D_SSM + LANES]
        cm = xbc[:, D_SSM + LANES:]
        x_dt = dt_ref[0, pl.ds(off, L), :] + dtb_ref[...]
        dt_all = jnp.maximum(x_dt, 0.0) + jnp.log(1.0 + jnp.exp(-jnp.abs(x_dt)))
        a_all = dt_all * a_row
        acs = jnp.dot(tri, a_all, precision=HIGHEST, preferred_element_type=F32)
        acs_t = acs.T
        bm_b = bm.astype(BF16)
        bm_t = bm.T.astype(BF16)
        cg = [jnp.where(low, cm, 0.0).astype(BF16), jnp.where(low, 0.0, cm).astype(BF16)]
        cb = [lax.dot_general(cg[g], bm_b, (((1,), (1,)), ((), ())), preferred_element_type=F32)
              for g in range(SSM_GROUPS)]
        ys = []
        for c in range(SSM_HEADS // 2):
            g = (2 * c) // (SSM_HEADS // SSM_GROUPS)
            ja, jb = d * SSM_HEADS + 2 * c, d * SSM_HEADS + 2 * c + 1
            sl = slice(c * LANES, (c + 1) * LANES)
            dt_pair = jnp.where(low, dt_all[:, ja:ja + 1], dt_all[:, jb:jb + 1])
            acs_pair = jnp.where(low, acs[:, ja:ja + 1], acs[:, jb:jb + 1])
            xdt = xs[:, sl] * dt_pair
            xdt_b = xdt.astype(BF16)
            y_heads = []
            for jh in (ja, jb):
                seg = acs[:, jh:jh + 1] - acs_t[jh:jh + 1, :]
                lmat = jnp.exp(jnp.where(causal, seg, -jnp.inf))
                y_heads.append(jnp.dot((cb[g] * lmat).astype(BF16), xdt_b, preferred_element_type=F32))
            y = jnp.where(low, y_heads[0], y_heads[1])
            st = st_ref[:, sl]
            y = y + jnp.dot(cg[g], st.astype(BF16), preferred_element_type=F32) * jnp.exp(acs_pair)
            ys.append(y)
            edge_row = acs_pair[edge:edge + 1, :]
            xdec = (xdt * jnp.exp(edge_row - acs_pair)).astype(BF16)
            st_ref[:, sl] = jnp.exp(edge_row) * st + jnp.dot(bm_t, xdec, preferred_element_type=F32)
        y = jnp.concatenate(ys, axis=1)
        if rev:
            o_ref[0, pl.ds(off, L), :] = y
        else:
            y = y + yb_ref[0, pl.ds(off, L), :] + dsk_ref[...] * xs
            y = y * _silu(z_ref[0, pl.ds(off, L), :])
            gw = D_SSM // SSM_GROUPS
            outs = []
            for g in range(SSM_GROUPS):
                yg = y[:, g * gw:(g + 1) * gw]
                outs.append(yg * lax.rsqrt(jnp.mean(yg * yg, axis=-1, keepdims=True) + EPS))
            o_ref[0, pl.ds(off, L), :] = (jnp.concatenate(outs, axis=1) * ng_ref[...]).astype(BF16)
        return carry

    lax.fori_loop(0, nchunk, chunk, 0)


def _ssd(xbc, dt, dtb, alog, extra, rev, rows):
    B, S, _ = xbc.shape
    nblk = S // rows
    blk = (lambda b, i: (b, nblk - 1 - i, 0)) if rev else (lambda b, i: (b, i, 0))
    const = lambda b, i: (0, 0)
    in_specs = [pl.BlockSpec((1, rows, D_XBC), blk),
                pl.BlockSpec((1, rows, LANES), blk),
                pl.BlockSpec((1, LANES), const),
                pl.BlockSpec((1, LANES), const)]
    args = [xbc, dt, dtb, alog]
    if not rev:
        yb, z, dsk, ng = extra
        in_specs += [pl.BlockSpec((1, rows, D_SSM), blk), pl.BlockSpec((1, rows, D_SSM), blk),
                     pl.BlockSpec((1, D_SSM), const), pl.BlockSpec((1, D_SSM), const)]
        args += [yb, z, dsk, ng]
    return pl.pallas_call(
        functools.partial(_ssd_kernel, rev=rev, rows=rows),
        grid=(B, nblk),
        in_specs=in_specs,
        out_specs=pl.BlockSpec((1, rows, D_SSM), blk),
        out_shape=jax.ShapeDtypeStruct((B, S, D_SSM), F32 if rev else BF16),
        scratch_shapes=[pltpu.VMEM((SSM_GROUPS * SSM_STATE, D_SSM), F32)],
        compiler_params=_params(("parallel", "arbitrary")),
        name="ssd_bwd" if rev else "ssd_fwd",
    )(*args)


def _outproj_kernel(a_ref, s_ref, x_ref, mod_ref, wa_ref, ws_ref, g2_ref, wr_ref, br_ref,
                    h_ref, n2_ref, eid_ref, wts_ref):
    mix = (jnp.dot(a_ref[0], wa_ref[...], preferred_element_type=F32)
           + jnp.dot(s_ref[0], ws_ref[...], preferred_element_type=F32))
    h = x_ref[0] + mod_ref[0, 2:3, :] * mix
    h_ref[0] = h
    hn = h * lax.rsqrt(jnp.mean(h * h, axis=-1, keepdims=True) + EPS) * g2_ref[...]
    n2 = hn * (1.0 + mod_ref[0, 4:5, :]) + mod_ref[0, 3:4, :]
    for s in range(ROW_TILES):
        n2_ref[0, :, s, :] = n2[:, s * LANES:(s + 1) * LANES]

    logits = jnp.dot(n2, wr_ref[...], precision=HIGHEST, preferred_element_type=F32) + br_ref[...]
    lane = lax.broadcasted_iota(jnp.int32, logits.shape, 1)
    lane_f = lane.astype(F32)
    big = float(LANES)
    is_g = lane < N_EGROUPS
    gl = jnp.where(is_g, logits, -jnp.inf)
    gmax = jnp.max(gl, axis=1, keepdims=True)
    gidx = jnp.min(jnp.where(gl == gmax, lane_f, big), axis=1, keepdims=True)
    g_w = 1.0 / jnp.sum(jnp.where(is_g, jnp.exp(gl - gmax), 0.0), axis=1, keepdims=True)
    e_lo = N_EGROUPS + EXPERTS_PER_GROUP * gidx
    sel = (lane_f >= e_lo) & (lane_f < e_lo + EXPERTS_PER_GROUP)
    el = jnp.where(sel, logits, -jnp.inf)
    v1 = jnp.max(el, axis=1, keepdims=True)
    i1 = jnp.min(jnp.where(el == v1, lane_f, big), axis=1, keepdims=True)
    el2 = jnp.where(lane_f == i1, -jnp.inf, el)
    v2 = jnp.max(el2, axis=1, keepdims=True)
    i2 = jnp.min(jnp.where(el2 == v2, lane_f, big), axis=1, keepdims=True)
    t = jnp.exp(v2 - v1)
    den = 1.0 + t
    w_a = g_w * (1.0 / den)
    w_b = g_w * (t / den)
    eid_ref[0] = jnp.where(lane == 0, i1 - N_EGROUPS, jnp.where(lane == 1, i2 - N_EGROUPS, 0.0)).astype(jnp.int32)
    wts_ref[0] = jnp.where(lane == 0, w_a, jnp.where(lane == 1, w_b, 0.0))


def _out_proj(attn, ssm, x, mod, wa, ws, g2, wr, br, tm):
    B, S, _ = x.shape
    row = lambda b, i: (b, i, 0)
    const = lambda b, i: (0, 0)
    return pl.pallas_call(
        _outproj_kernel,
        grid=(B, S // tm),
        in_specs=[pl.BlockSpec((1, tm, D_ATTN), row),
                  pl.BlockSpec((1, tm, D_SSM), row),
                  pl.BlockSpec((1, tm, D_MODEL), row),
                  pl.BlockSpec((1, N_MOD, D_MODEL), lambda b, i: (b, 0, 0)),
                  pl.BlockSpec((D_ATTN, D_MODEL), const),
                  pl.BlockSpec((D_SSM, D_MODEL), const),
                  pl.BlockSpec((1, D_MODEL), const),
                  pl.BlockSpec((D_MODEL, LANES), const),
                  pl.BlockSpec((1, LANES), const)],
        out_specs=[pl.BlockSpec((1, tm, D_MODEL), row),
                   pl.BlockSpec((1, tm, ROW_TILES, LANES), lambda b, i: (b, i, 0, 0)),
                   pl.BlockSpec((1, tm, LANES), row), pl.BlockSpec((1, tm, LANES), row)],
        out_shape=[jax.ShapeDtypeStruct((B, S, D_MODEL), F32),
                   jax.ShapeDtypeStruct((B, S, ROW_TILES, LANES), F32),
                   jax.ShapeDtypeStruct((B, S, LANES), jnp.int32), jax.ShapeDtypeStruct((B, S, LANES), F32)],
        compiler_params=_params(("parallel", "parallel")),
        name="out_proj",
    )(attn, ssm, x, mod, wa, ws, g2, wr, br)


def _rank_kernel(eid_ref, rank_ref, cnt_ref, carry_ref, *, tt):
    @pl.when(pl.program_id(0) == 0)
    def _():
        carry_ref[...] = jnp.zeros(carry_ref.shape, F32)

    eid = eid_ref[...]
    lane = lax.broadcasted_iota(jnp.int32, eid.shape, 1)
    r_i = lax.broadcasted_iota(jnp.int32, (tt, tt), 0)
    c_i = lax.broadcasted_iota(jnp.int32, (tt, tt), 1)
    before = (c_i < r_i).astype(BF16)
    carry = carry_ref[0:1, :]
    ranks = []
    for k in range(2):
        oh = lane == eid[:, k:k + 1]
        oh_f = oh.astype(F32)
        pref = jnp.dot(before, oh.astype(BF16), preferred_element_type=F32) + carry
        ranks.append(jnp.sum(oh_f * pref, axis=1, keepdims=True))
        carry = carry + jnp.sum(oh_f, axis=0, keepdims=True)
    carry_ref[0:1, :] = carry
    rank_ref[...] = jnp.where(lane == 0, ranks[0], jnp.where(lane == 1, ranks[1], 0.0)).astype(jnp.int32)
    cnt_ref[...] = jnp.broadcast_to(carry, cnt_ref.shape).astype(jnp.int32)


def _rank(eid, tt):
    T = eid.shape[0]
    return pl.pallas_call(
        functools.partial(_rank_kernel, tt=tt),
        grid=(T // tt,),
        in_specs=[pl.BlockSpec((tt, LANES), lambda i: (i, 0))],
        out_specs=[pl.BlockSpec((tt, LANES), lambda i: (i, 0)),
                   pl.BlockSpec((SUBLANES, LANES), lambda i: (0, 0))],
        out_shape=[jax.ShapeDtypeStruct((T, LANES), jnp.int32),
                   jax.ShapeDtypeStruct((SUBLANES, LANES), jnp.int32)],
        scratch_shapes=[pltpu.VMEM((SUBLANES, LANES), F32)],
        compiler_params=_params(("arbitrary",)),
        name="rank",
    )(eid)


def _row_copy(src, dst, sem):
    return pltpu.make_async_copy(src, dst, sem)


def _dispatch_kernel(dest_hbm, n2_ref, xin_in, xin_hbm, idx_ref, idx_sem, row_sem):
    del xin_in
    i = pl.program_id(0)
    idx_cp = pltpu.make_async_copy(dest_hbm.at[pl.ds(pl.multiple_of(i * SMEM_IDX, SMEM_IDX), SMEM_IDX)],
                                   idx_ref, idx_sem)
    idx_cp.start()
    idx_cp.wait()

    def issue(t, carry):
        for k in range(2):
            _row_copy(n2_ref.at[t], xin_hbm.at[idx_ref[2 * t + k]], row_sem).start()
        return carry

    lax.fori_loop(0, TOK_TILE, issue, 0, unroll=8)

    def drain(t, carry):
        for k in range(2):
            _row_copy(n2_ref.at[0], xin_hbm.at[0], row_sem).wait()
        return carry

    lax.fori_loop(0, TOK_TILE, drain, 0, unroll=8)


def _dispatch(dest, n2_rows, xin_zero):
    T = n2_rows.shape[0]
    any_spec = pl.BlockSpec(memory_space=pl.ANY)
    return pl.pallas_call(
        _dispatch_kernel,
        grid=(T // TOK_TILE,),
        in_specs=[any_spec, pl.BlockSpec((TOK_TILE, ROW_TILES, LANES), lambda i: (i, 0, 0)), any_spec],
        out_specs=any_spec,
        out_shape=jax.ShapeDtypeStruct(xin_zero.shape, F32),
        scratch_shapes=[pltpu.SMEM((SMEM_IDX,), jnp.int32), pltpu.SemaphoreType.DMA, pltpu.SemaphoreType.DMA],
        input_output_aliases={2: 0},
        compiler_params=pltpu.CompilerParams(dimension_semantics=("arbitrary",), has_side_effects=True),
        name="dispatch",
    )(dest, n2_rows, xin_zero)


def _moe_kernel(beid_ref, nused_ref, xin_ref, w1_ref, w3_ref, w2_ref, y_ref):
    del beid_ref
    i = pl.program_id(0)

    @pl.when(i < nused_ref[0])
    def _():
        x = jnp.concatenate([xin_ref[:, s, :] for s in range(ROW_TILES)], axis=1).astype(BF16)
        h1 = jnp.dot(x, w1_ref[0], preferred_element_type=F32)
        h3 = jnp.dot(x, w3_ref[0], preferred_element_type=F32)
        y = jnp.dot((_silu(h1) * h3).astype(BF16), w2_ref[0], preferred_element_type=F32)
        for s in range(ROW_TILES):
            y_ref[:, s, :] = y[:, s * LANES:(s + 1) * LANES]

    @pl.when(i >= nused_ref[0])
    def _():
        y_ref[...] = jnp.zeros(y_ref.shape, F32)


def _moe(blk_eid, n_used, xin, w1, w3, w2):
    R = xin.shape[0]
    grid_spec = pltpu.PrefetchScalarGridSpec(
        num_scalar_prefetch=2,
        grid=(R // MOE_ROWS,),
        in_specs=[pl.BlockSpec((MOE_ROWS, ROW_TILES, LANES), lambda i, be, nu: (i, 0, 0)),
                  pl.BlockSpec((1, D_MODEL, D_EXPERT), lambda i, be, nu: (be[i], 0, 0)),
                  pl.BlockSpec((1, D_MODEL, D_EXPERT), lambda i, be, nu: (be[i], 0, 0)),
                  pl.BlockSpec((1, D_EXPERT, D_MODEL), lambda i, be, nu: (be[i], 0, 0))],
        out_specs=pl.BlockSpec((MOE_ROWS, ROW_TILES, LANES), lambda i, be, nu: (i, 0, 0)),
    )
    return pl.pallas_call(
        _moe_kernel,
        grid_spec=grid_spec,
        out_shape=jax.ShapeDtypeStruct(xin.shape, F32),
        compiler_params=_params(("arbitrary",)),
        name="moe",
    )(blk_eid, n_used, xin, w1, w3, w2)


def _combine_kernel(dest_hbm, y_hbm, h_ref, wts_ref, mod_ref, fg_ref, o_ref, idx_ref, buf_ref, idx_sem, row_sem,
                    *, tiles_per_seq):
    i = pl.program_id(0) * tiles_per_seq + pl.program_id(1)
    idx_cp = pltpu.make_async_copy(dest_hbm.at[pl.ds(pl.multiple_of(i * SMEM_IDX, SMEM_IDX), SMEM_IDX)],
                                   idx_ref, idx_sem)
    idx_cp.start()
    idx_cp.wait()

    def issue(t, carry):
        for k in range(2):
            _row_copy(y_hbm.at[idx_ref[2 * t + k]], buf_ref.at[k, t], row_sem).start()
        return carry

    lax.fori_loop(0, TOK_TILE, issue, 0)

    def drain(t, carry):
        for k in range(2):
            _row_copy(y_hbm.at[0], buf_ref.at[k, 0], row_sem).wait()
        return carry

    lax.fori_loop(0, TOK_TILE, drain, 0)

    wts = wts_ref[0]
    ffn = jnp.zeros((TOK_TILE, D_MODEL), F32)
    for k in range(2):
        yk = jnp.concatenate([buf_ref[k, :, s, :] for s in range(ROW_TILES)], axis=1)
        ffn = ffn + wts[:, k:k + 1] * yk
    h = h_ref[0] + mod_ref[0, 5:6, :] * ffn
    o_ref[0] = h * lax.rsqrt(jnp.mean(h * h, axis=-1, keepdims=True) + EPS) * fg_ref[...]


def _combine(dest, y_rows, h, wts, mod, fg):
    B, S, _ = h.shape
    tps = S // TOK_TILE
    row = lambda b, i: (b, i, 0)
    any_spec = pl.BlockSpec(memory_space=pl.ANY)
    return pl.pallas_call(
        functools.partial(_combine_kernel, tiles_per_seq=tps),
        grid=(B, tps),
        in_specs=[any_spec, any_spec,
                  pl.BlockSpec((1, TOK_TILE, D_MODEL), row),
                  pl.BlockSpec((1, TOK_TILE, LANES), row),
                  pl.BlockSpec((1, N_MOD, D_MODEL), lambda b, i: (b, 0, 0)),
                  pl.BlockSpec((1, D_MODEL), lambda b, i: (0, 0))],
        out_specs=pl.BlockSpec((1, TOK_TILE, D_MODEL), row),
        out_shape=jax.ShapeDtypeStruct((B, S, D_MODEL), F32),
        scratch_shapes=[pltpu.SMEM((SMEM_IDX,), jnp.int32),
                        pltpu.VMEM((2, TOK_TILE, ROW_TILES, LANES), F32),
                        pltpu.SemaphoreType.DMA, pltpu.SemaphoreType.DMA],
        compiler_params=_params(("arbitrary", "arbitrary")),
        name="combine",
    )(dest, y_rows, h, wts, mod, fg)


def _rope_tables(S):
    half = HEAD_DIM // 4
    inv = ROPE_THETA ** (-jnp.arange(half, dtype=F32) / half)
    t = jnp.arange(S)
    pos = jnp.stack([(t // GRID_W).astype(F32), (t % GRID_W).astype(F32)], axis=1)
    ang = pos[:, :, None] * inv[None, None, :]
    cos = jnp.cos(ang)
    sin = jnp.sin(ang)
    cos_h = jnp.concatenate([cos, cos], axis=-1).reshape(S, HEAD_DIM)
    sin_h = jnp.concatenate([-sin, sin], axis=-1).reshape(S, HEAD_DIM)
    reps = LANES // HEAD_DIM
    return jnp.tile(cos_h, (1, reps)), jnp.tile(sin_h, (1, reps))


def _moe_ffn(n2, h1, eid, wts, mod, w1, w3, w2, final_g):
    B, S, _ = h1.shape
    T = B * S
    n_blocks = (2 * T + MOE_ROWS - 1) // MOE_ROWS + N_EXPERTS
    R = n_blocks * MOE_ROWS
    eid2 = eid.reshape(T, LANES)
    rank, counts = _rank(eid2, 512)
    counts = counts[0, :N_EXPERTS]
    padded = (counts + MOE_ROWS - 1) // MOE_ROWS * MOE_ROWS
    pad_ends = jnp.cumsum(padded)
    pad_starts = pad_ends - padded
    e2 = eid2[:, :2]
    onehot = e2[:, :, None] == jnp.arange(N_EXPERTS, dtype=jnp.int32)[None, None, :]
    dest = (jnp.sum(jnp.where(onehot, pad_starts[None, None, :], 0), axis=-1) + rank[:, :2]).reshape(2 * T)
    blk_eid = jnp.minimum(
        jnp.searchsorted(pad_ends, jnp.arange(n_blocks, dtype=jnp.int32) * MOE_ROWS, side="right"),
        N_EXPERTS - 1).astype(jnp.int32)
    n_used = (pad_ends[-1:] // MOE_ROWS).astype(jnp.int32)
    xin = _dispatch(dest, n2.reshape(T, ROW_TILES, LANES), jnp.zeros((R, ROW_TILES, LANES), F32))
    y_rows = _moe(blk_eid, n_used, xin, w1, w3, w2)
    return _combine(dest, y_rows, h1, wts, mod, final_g)


def _trunk(x, mod, p):
    B, S, _ = x.shape
    cos, sin = _rope_tables(S)
    q, k, v, z, xbc, dtr = _in_proj(x, mod, p["g1"], p["w_in"], p["qg"], p["kg"], cos, sin, p["seg"], tm=512)
    attn = _attention(q, k, v, p["attn_g"], tq=128, tk=512)
    xact = _conv(xbc, p["conv_w"], p["conv_b"], tr=512)
    y_b = _ssd(xact, dtr, p["dt_bias"], p["a_log"], None, rev=True, rows=512)
    ssm = _ssd(xact, dtr, p["dt_bias"], p["a_log"], (y_b, z, p["d_skip"], p["ssm_g"]), rev=False, rows=512)
    h1, n2, eid, wts = _out_proj(attn, ssm, x, mod, p["w_out_a"], p["w_out_s"], p["g2"], p["w_r"], p["b_r"], tm=512)
    return _moe_ffn(n2, h1, eid, wts, mod, p["w1"], p["w3"], p["w2"], p["final_g"])


def kernel(x_prompt, x_sample, c_prompt, c_sample, w_ada, b_ada, norm1_g, w_in, q_norm_g, k_norm_g, attn_out_g,
           conv_w, conv_b, a_log, dt_bias, d_skip, ssm_norm_g, w_out, norm2_g, w_rg, b_rg, w_re, b_re, w1, w3, w2,
           final_g):
    l = 0
    Bp, Bs = c_prompt.shape[0], c_sample.shape[0]
    rows = (Bp + Bs + SUBLANES - 1) // SUBLANES * SUBLANES
    c_all = jnp.zeros((rows, D_MODEL), F32).at[:Bp].set(c_prompt).at[Bp:Bp + Bs].set(c_sample)
    mod = _modulation(c_all, w_ada[l], b_ada[l].reshape(1, -1))
    mod_p = mod[:Bp].reshape(Bp, N_MOD, D_MODEL)
    mod_s = mod[Bp:Bp + Bs].reshape(Bs, N_MOD, D_MODEL)

    def lane_pad(a):
        return jnp.pad(a.reshape(1, -1), ((0, 0), (0, LANES - a.size)))

    head_id = jnp.arange(LANES) // HEAD_DIM
    w_r = jnp.concatenate([w_rg[l]] + [w_re[l, g] for g in range(N_EGROUPS)], axis=1)
    b_r = jnp.concatenate([b_rg[l]] + [b_re[l, g] for g in range(N_EGROUPS)])
    p = {
        "g1": norm1_g[l].reshape(1, -1),
        "w_in": jnp.pad(w_in[l], ((0, 0), (0, D_IN_PAD - D_IN_PROJ))).astype(BF16),
        "qg": jnp.tile(q_norm_g[l] * (HEAD_DIM ** -0.5), N_HEADS).reshape(1, -1),
        "kg": jnp.tile(k_norm_g[l], N_KV_HEADS).reshape(1, -1),
        "seg": (head_id[:, None] == head_id[None, :]).astype(BF16),
        "attn_g": attn_out_g[l].reshape(1, -1),
        "conv_w": conv_w[l].reshape(CONV_K, D_XBC),
        "conv_b": conv_b[l].reshape(1, -1),
        "dt_bias": lane_pad(dt_bias[l]),
        "a_log": lane_pad(a_log[l]),
        "d_skip": jnp.repeat(d_skip[l], SSM_HEAD_DIM).reshape(1, -1),
        "ssm_g": ssm_norm_g[l].reshape(1, -1),
        "w_out_a": w_out[l, :D_ATTN].astype(BF16),
        "w_out_s": w_out[l, D_ATTN:].astype(BF16),
        "g2": norm2_g[l].reshape(1, -1),
        "w_r": jnp.pad(w_r, ((0, 0), (0, LANES - w_r.shape[1]))),
        "b_r": lane_pad(b_r),
        "w1": w1[l].astype(BF16),
        "w3": w3[l].astype(BF16),
        "w2": w2[l].astype(BF16),
        "final_g": final_g.reshape(1, -1),
    }
    return _trunk(x_prompt, mod_p, p), _trunk(x_sample, mod_s, p)
```

```python
import functools

import jax
import jax.numpy as jnp
from jax import lax
from jax.experimental import pallas as pl
from jax.experimental.pallas import tpu as pltpu

F32 = jnp.float32
BF16 = jnp.bfloat16
HIGHEST = lax.Precision.HIGHEST

D_MODEL = 1024
GRID_W = 64
EPS = 1e-6
N_HEADS = 8
N_KV_HEADS = 2
HEAD_DIM = 64
D_ATTN = N_HEADS * HEAD_DIM
D_KV = N_KV_HEADS * HEAD_DIM
ROPE_THETA = 10000.0
SSM_HEADS = 8
SSM_HEAD_DIM = 64
D_SSM = SSM_HEADS * SSM_HEAD_DIM
SSM_STATE = 64
SSM_GROUPS = 2
CONV_K = 5
CHUNK = 128
D_XBC = D_SSM + 2 * SSM_GROUPS * SSM_STATE
D_IN_PROJ = D_ATTN + 2 * D_KV + 2 * D_SSM + 2 * SSM_GROUPS * SSM_STATE + 2 * SSM_HEADS
N_EGROUPS = 4
EXPERTS_PER_GROUP = 8
N_EXPERTS = N_EGROUPS * EXPERTS_PER_GROUP
D_EXPERT = 512
N_MOD = 6

LANES = 128
SUBLANES = 8
ROW_TILES = D_MODEL // LANES
D_IN_PAD = (D_IN_PROJ + LANES - 1) // LANES * LANES
SMEM_IDX = 1024
TOK_TILE = SMEM_IDX // 2
MOE_ROWS = 256
VMEM_LIMIT = 48 * 1024 * 1024

_Q0, _K0, _V0, _Z0, _X0, _DT0 = 0, D_ATTN, D_ATTN + D_KV, D_ATTN + 2 * D_KV, D_ATTN + 2 * D_KV + D_SSM, \
    D_ATTN + 2 * D_KV + D_SSM + D_XBC


def _params(sem):
    return pltpu.CompilerParams(dimension_semantics=sem, vmem_limit_bytes=VMEM_LIMIT)


def _silu(x):
    return x * jax.nn.sigmoid(x)


def _mod_kernel(c_ref, w_ref, b_ref, o_ref):
    o_ref[...] = jnp.dot(_silu(c_ref[...]), w_ref[...], precision=HIGHEST,
                         preferred_element_type=F32) + b_ref[...]


def _modulation(c, w, b):
    rows, n = c.shape[0], w.shape[1]
    tn = 1024
    return pl.pallas_call(
        _mod_kernel,
        grid=(n // tn,),
        in_specs=[pl.BlockSpec((rows, D_MODEL), lambda j: (0, 0)),
                  pl.BlockSpec((D_MODEL, tn), lambda j: (0, j)),
                  pl.BlockSpec((1, tn), lambda j: (0, j))],
        out_specs=pl.BlockSpec((rows, tn), lambda j: (0, j)),
        out_shape=jax.ShapeDtypeStruct((rows, n), F32),
        compiler_params=_params(("arbitrary",)),
        name="mod",
    )(c, w, b)


def _inproj_kernel(x_ref, mod_ref, g1_ref, w_ref, qg_ref, kg_ref, cos_ref, sin_ref, seg_ref,
                   q_ref, k_ref, v_ref, z_ref, xbc_ref, dt_ref):
    x = x_ref[0]
    ms = jnp.mean(x * x, axis=-1, keepdims=True)
    xn = x * lax.rsqrt(ms + EPS) * g1_ref[...]
    n1 = xn * (1.0 + mod_ref[0, 1:2, :]) + mod_ref[0, 0:1, :]
    proj = jnp.dot(n1.astype(BF16), w_ref[...], preferred_element_type=F32)

    cos = cos_ref[...]
    sin = sin_ref[...]
    seg = seg_ref[...]
    lane = lax.broadcasted_iota(jnp.int32, cos.shape, 1)
    low_half = (lane % 32) < 16

    def norm_rope(xc, gain):
        sq = xc * xc
        hi = sq.astype(BF16)
        lo = (sq - hi.astype(F32)).astype(BF16)
        ss = (jnp.dot(hi, seg, preferred_element_type=F32)
              + jnp.dot(lo, seg, preferred_element_type=F32))
        xg = xc * lax.rsqrt(ss * (1.0 / HEAD_DIM) + EPS) * gain
        rot = jnp.where(low_half, pltpu.roll(xg, LANES - 16, 1), pltpu.roll(xg, 16, 1))
        return xg * cos + rot * sin

    for c in range(D_ATTN // LANES):
        sl = slice(c * LANES, (c + 1) * LANES)
        q_ref[0, :, sl] = norm_rope(proj[:, _Q0 + c * LANES:_Q0 + (c + 1) * LANES], qg_ref[:, sl]).astype(BF16)
    k_ref[0] = norm_rope(proj[:, _K0:_V0], kg_ref[...]).astype(BF16)
    v = proj[:, _V0:_Z0]
    for j in range(N_KV_HEADS):
        mine = (lane >= HEAD_DIM * j) & (lane < HEAD_DIM * (j + 1))
        v_ref[0, j] = jnp.where(mine, v, 1.0).astype(BF16)
    z_ref[0] = proj[:, _Z0:_X0]
    xbc_ref[0] = proj[:, _X0:_DT0]
    dt_ref[0] = proj[:, _DT0:D_IN_PAD]


def _in_proj(x, mod, g1, w_in, qg, kg, cos, sin, seg, tm):
    B, S, _ = x.shape
    row = lambda b, i: (b, i, 0)
    const = lambda b, i: (0, 0)
    outs = [(D_ATTN, BF16), (D_KV, BF16), None, (D_SSM, F32), (D_XBC, F32), (LANES, F32)]
    out_specs = [pl.BlockSpec((1, tm, o[0]), row) if o else
                 pl.BlockSpec((1, N_KV_HEADS, tm, D_KV), lambda b, i: (b, 0, i, 0)) for o in outs]
    out_shape = [jax.ShapeDtypeStruct((B, S, o[0]), o[1]) if o else
                 jax.ShapeDtypeStruct((B, N_KV_HEADS, S, D_KV), BF16) for o in outs]
    return pl.pallas_call(
        _inproj_kernel,
        grid=(B, S // tm),
        in_specs=[pl.BlockSpec((1, tm, D_MODEL), row),
                  pl.BlockSpec((1, N_MOD, D_MODEL), lambda b, i: (b, 0, 0)),
                  pl.BlockSpec((1, D_MODEL), const),
                  pl.BlockSpec((D_MODEL, D_IN_PAD), const),
                  pl.BlockSpec((1, D_ATTN), const),
                  pl.BlockSpec((1, D_KV), const),
                  pl.BlockSpec((tm, LANES), lambda b, i: (i, 0)),
                  pl.BlockSpec((tm, LANES), lambda b, i: (i, 0)),
                  pl.BlockSpec((LANES, LANES), const)],
        out_specs=out_specs,
        out_shape=out_shape,
        compiler_params=_params(("parallel", "parallel")),
        name="in_proj",
    )(x, mod, g1, w_in, qg, kg, cos, sin, seg)


def _attn_kernel(q_ref, k_ref, v_ref, g_ref, o_ref, qs_ref, m_ref, acc_ref, *, tq, tk, nk):
    G = N_HEADS // N_KV_HEADS
    lane = lax.broadcasted_iota(jnp.int32, (tq, LANES), 1)
    qf = q_ref[0].astype(F32)
    for j in range(N_KV_HEADS):
        pieces = []
        for g in range(G):
            h = G * j + g
            chunk = qf[:, (h // 2) * LANES:(h // 2 + 1) * LANES]
            if h % 2 != j:
                chunk = pltpu.roll(chunk, HEAD_DIM, 1)
            keep = (lane >= HEAD_DIM * j) & (lane < HEAD_DIM * (j + 1))
            pieces.append(jnp.where(keep, chunk, 0.0))
        qs_ref[j] = jnp.concatenate(pieces, axis=0).astype(BF16)
    m_ref[...] = jnp.full(m_ref.shape, -jnp.inf, F32)
    acc_ref[...] = jnp.zeros(acc_ref.shape, F32)

    def body(c, carry):
        off = pl.multiple_of(c * tk, tk)
        kc = k_ref[0, pl.ds(off, tk), :]
        for j in range(N_KV_HEADS):
            s = lax.dot_general(qs_ref[j], kc, (((1,), (1,)), ((), ())), preferred_element_type=F32)
            m_prev = m_ref[j]
            m_new = jnp.maximum(m_prev, jnp.max(s, axis=1, keepdims=True))
            alpha = jnp.exp(m_prev - m_new)
            p = jnp.exp(s - jnp.concatenate([m_new] * (tk // LANES), axis=1))
            acc_ref[j] = alpha * acc_ref[j] + jnp.dot(p.astype(BF16), v_ref[0, j, pl.ds(off, tk), :],
                                                      preferred_element_type=F32)
            m_ref[j] = m_new
        return carry

    lax.fori_loop(0, nk, body, 0, unroll=2)

    o = [acc_ref[j] / pltpu.roll(acc_ref[j], HEAD_DIM, 1) for j in range(N_KV_HEADS)]

    def head(h, half):
        j, g = divmod(h, G)
        piece = o[j][g * tq:(g + 1) * tq, :]
        return pltpu.roll(piece, HEAD_DIM, 1) if j != half else piece

    of = jnp.concatenate(
        [jnp.where(lane < HEAD_DIM, head(2 * c, 0), head(2 * c + 1, 1)) for c in range(N_HEADS // 2)], axis=1)
    ms = jnp.mean(of * of, axis=-1, keepdims=True)
    o_ref[0] = (of * lax.rsqrt(ms + EPS) * g_ref[...]).astype(BF16)


def _attention(q, k, v, g, tq, tk):
    B, S, _ = q.shape
    G = N_HEADS // N_KV_HEADS
    kern = functools.partial(_attn_kernel, tq=tq, tk=tk, nk=S // tk)
    return pl.pallas_call(
        kern,
        grid=(B, S // tq),
        in_specs=[pl.BlockSpec((1, tq, D_ATTN), lambda b, i: (b, i, 0)),
                  pl.BlockSpec((1, S, D_KV), lambda b, i: (b, 0, 0)),
                  pl.BlockSpec((1, N_KV_HEADS, S, D_KV), lambda b, i: (b, 0, 0, 0)),
                  pl.BlockSpec((1, D_ATTN), lambda b, i: (0, 0))],
        out_specs=pl.BlockSpec((1, tq, D_ATTN), lambda b, i: (b, i, 0)),
        out_shape=jax.ShapeDtypeStruct((B, S, D_ATTN), BF16),
        scratch_shapes=[pltpu.VMEM((N_KV_HEADS, G * tq, LANES), BF16),
                        pltpu.VMEM((N_KV_HEADS, G * tq, LANES), F32),
                        pltpu.VMEM((N_KV_HEADS, G * tq, LANES), F32)],
        compiler_params=_params(("parallel", "parallel")),
        name="attn",
    )(q, k, v, g)


def _conv_kernel(xp_ref, xc_ref, xn_ref, w_ref, b_ref, o_ref, ext_ref, *, tr, nblk):
    i = pl.program_id(1)
    pad = CONV_K // 2
    ext_ref[0:SUBLANES, :] = jnp.where(i > 0, xp_ref[0], 0.0)
    ext_ref[SUBLANES:SUBLANES + tr, :] = xc_ref[0]
    ext_ref[SUBLANES + tr:, :] = jnp.where(i < nblk - 1, xn_ref[0], 0.0)
    acc = jnp.zeros((tr, D_XBC), F32) + b_ref[...]
    for kk in range(CONV_K):
        acc = acc + w_ref[kk:kk + 1, :] * ext_ref[SUBLANES - pad + kk:SUBLANES - pad + kk + tr, :]
    o_ref[0] = _silu(acc)


def _conv(xbc, w, b, tr):
    B, S, _ = xbc.shape
    nblk = S // tr
    per = tr // SUBLANES
    kern = functools.partial(_conv_kernel, tr=tr, nblk=nblk)
    return pl.pallas_call(
        kern,
        grid=(B, nblk),
        in_specs=[pl.BlockSpec((1, SUBLANES, D_XBC), lambda b, i: (b, jnp.maximum(i * per - 1, 0), 0)),
                  pl.BlockSpec((1, tr, D_XBC), lambda b, i: (b, i, 0)),
                  pl.BlockSpec((1, SUBLANES, D_XBC),
                               lambda b, i: (b, jnp.minimum((i + 1) * per, S // SUBLANES - 1), 0)),
                  pl.BlockSpec((CONV_K, D_XBC), lambda b, i: (0, 0)),
                  pl.BlockSpec((1, D_XBC), lambda b, i: (0, 0))],
        out_specs=pl.BlockSpec((1, tr, D_XBC), lambda b, i: (b, i, 0)),
        out_shape=jax.ShapeDtypeStruct((B, S, D_XBC), F32),
        scratch_shapes=[pltpu.VMEM((tr + 2 * SUBLANES, D_XBC), F32)],
        compiler_params=_params(("parallel", "parallel")),
        name="conv",
    )(xbc, xbc, xbc, w, b)


def _ssd_kernel(*refs, rev, rows):
    if rev:
        xbc_ref, dt_ref, dtb_ref, alog_ref, o_ref, st_ref = refs
    else:
        xbc_ref, dt_ref, dtb_ref, alog_ref, yb_ref, z_ref, dsk_ref, ng_ref, o_ref, st_ref = refs
    L = CHUNK
    d = 1 if rev else 0
    edge = 0 if rev else L - 1
    nchunk = rows // L

    @pl.when(pl.program_id(1) == 0)
    def _():
        st_ref[...] = jnp.zeros(st_ref.shape, F32)

    r_i = lax.broadcasted_iota(jnp.int32, (L, L), 0)
    c_i = lax.broadcasted_iota(jnp.int32, (L, L), 1)
    causal = (c_i >= r_i) if rev else (r_i >= c_i)
    tri = causal.astype(F32)
    low = c_i < SSM_HEAD_DIM
    a_row = -jnp.exp(alog_ref[...])

    def chunk(ci, carry):
        cc = (nchunk - 1 - ci) if rev else ci
        off = pl.multiple_of(cc * L, L)
        xbc = xbc_ref[0, pl.ds(off, L), :]
        xs = xbc[:, :D_SSM]
        bm = xbc[:, D_SSM:D_SSM + LANES]
        cm = xbc[:, D_SSM + LANES:]
        x_dt = dt_ref[0, pl.ds(off, L), :] + dtb_ref[...]
        dt_all = jnp.maximum(x_dt, 0.0) + jnp.log(1.0 + jnp.exp(-jnp.abs(x_dt)))
        a_all = dt_all * a_row
        acs = jnp.dot(tri, a_all, precision=HIGHEST, preferred_element_type=F32)
        acs_t = acs.T
        bm_b = bm.astype(BF16)
        bm_t = bm.T.astype(BF16)
        cg = [jnp.where(low, cm, 0.0).astype(BF16), jnp.where(low, 0.0, cm).astype(BF16)]
        cb = [lax.dot_general(cg[g], bm_b, (((1,), (1,)), ((), ())), preferred_element_type=F32)
              for g in range(SSM_GROUPS)]
        ys = []
        for c in range(SSM_HEADS // 2):
            g = (2 * c) // (SSM_HEADS // SSM_GROUPS)
            ja, jb = d * SSM_HEADS + 2 * c, d * SSM_HEADS + 2 * c + 1
            sl = slice(c * LANES, (c + 1) * LANES)
            dt_pair = jnp.where(low, dt_all[:, ja:ja + 1], dt_all[:, jb:jb + 1])
            acs_pair = jnp.where(low, acs[:, ja:ja + 1], acs[:, jb:jb + 1])
            xdt = xs[:, sl] * dt_pair
            xdt_b = xdt.astype(BF16)
            y_heads = []
            for jh in (ja, jb):
                seg = acs[:, jh:jh + 1] - acs_t[jh:jh + 1, :]
                lmat = jnp.exp(jnp.where(causal, seg, -jnp.inf))
                y_heads.append(jnp.dot((cb[g] * lmat).astype(BF16), xdt_b, preferred_element_type=F32))
            y = jnp.where(low, y_heads[0], y_heads[1])
            st = st_ref[:, sl]
            y = y + jnp.dot(cg[g], st.astype(BF16), preferred_element_type=F32) * jnp.exp(acs_pair)
            ys.append(y)
            edge_row = acs_pair[edge:edge + 1, :]
            xdec = (xdt * jnp.exp(edge_row - acs_pair)).astype(BF16)
            st_ref[:, sl] = jnp.exp(edge_row) * st + jnp.dot(bm_t, xdec, preferred_element_type=F32)
        y = jnp.concatenate(ys, axis=1)
        if rev:
            o_ref[0, pl.ds(off, L), :] = y
        else:
            y = y + yb_ref[0, pl.ds(off, L), :] + dsk_ref[...] * xs
            y = y * _silu(z_ref[0, pl.ds(off, L), :])
            gw = D_SSM // SSM_GROUPS
            outs = []
            for g in range(SSM_GROUPS):
                yg = y[:, g * gw:(g + 1) * gw]
                outs.append(yg * lax.rsqrt(jnp.mean(yg * yg, axis=-1, keepdims=True) + EPS))
            o_ref[0, pl.ds(off, L), :] = (jnp.concatenate(outs, axis=1) * ng_ref[...]).astype(BF16)
        return carry

    lax.fori_loop(0, nchunk, chunk, 0)


def _ssd(xbc, dt, dtb, alog, extra, rev, rows):
    B, S, _ = xbc.shape
    nblk = S // rows
    blk = (lambda b, i: (b, nblk - 1 - i, 0)) if rev else (lambda b, i: (b, i, 0))
    const = lambda b, i: (0, 0)
    in_specs = [pl.BlockSpec((1, rows, D_XBC), blk),
                pl.BlockSpec((1, rows, LANES), blk),
                pl.BlockSpec((1, LANES), const),
                pl.BlockSpec((1, LANES), const)]
    args = [xbc, dt, dtb, alog]
    if not rev:
        yb, z, dsk, ng = extra
        in_specs += [pl.BlockSpec((1, rows, D_SSM), blk), pl.BlockSpec((1, rows, D_SSM), blk),
                     pl.BlockSpec((1, D_SSM), const), pl.BlockSpec((1, D_SSM), const)]
        args += [yb, z, dsk, ng]
    return pl.pallas_call(
        functools.partial(_ssd_kernel, rev=rev, rows=rows),
        grid=(B, nblk),
        in_specs=in_specs,
        out_specs=pl.BlockSpec((1, rows, D_SSM), blk),
        out_shape=jax.ShapeDtypeStruct((B, S, D_SSM), F32 if rev else BF16),
        scratch_shapes=[pltpu.VMEM((SSM_GROUPS * SSM_STATE, D_SSM), F32)],
        compiler_params=_params(("parallel", "arbitrary")),
        name="ssd_bwd" if rev else "ssd_fwd",
    )(*args)


def _outproj_kernel(a_ref, s_ref, x_ref, mod_ref, wa_ref, ws_ref, g2_ref, wrh_ref, wrl_ref, br_ref,
                    h_ref, n2_ref, eid_ref, wts_ref):
    mix = (jnp.dot(a_ref[0], wa_ref[...], preferred_element_type=F32)
           + jnp.dot(s_ref[0], ws_ref[...], preferred_element_type=F32))
    h = x_ref[0] + mod_ref[0, 2:3, :] * mix
    h_ref[0] = h
    hn = h * lax.rsqrt(jnp.mean(h * h, axis=-1, keepdims=True) + EPS) * g2_ref[...]
    n2 = hn * (1.0 + mod_ref[0, 4:5, :]) + mod_ref[0, 3:4, :]
    for s in range(ROW_TILES):
        n2_ref[0, pl.ds(s, n2.shape[0], stride=ROW_TILES), :] = n2[:, s * LANES:(s + 1) * LANES]

    n_hi = n2.astype(BF16)
    n_lo = (n2 - n_hi.astype(F32)).astype(BF16)
    logits = (jnp.dot(n_hi, wrh_ref[...], preferred_element_type=F32)
              + (jnp.dot(n_lo, wrh_ref[...], preferred_element_type=F32)
                 + jnp.dot(n_hi, wrl_ref[...], preferred_element_type=F32))) + br_ref[...]
    lane = lax.broadcasted_iota(jnp.int32, logits.shape, 1)
    lane_f = lane.astype(F32)
    big = float(LANES)
    is_g = lane < N_EGROUPS
    gl = jnp.where(is_g, logits, -jnp.inf)
    gmax = jnp.max(gl, axis=1, keepdims=True)
    gidx = jnp.min(jnp.where(gl == gmax, lane_f, big), axis=1, keepdims=True)
    g_w = 1.0 / jnp.sum(jnp.where(is_g, jnp.exp(gl - gmax), 0.0), axis=1, keepdims=True)
    e_lo = N_EGROUPS + EXPERTS_PER_GROUP * gidx
    sel = (lane_f >= e_lo) & (lane_f < e_lo + EXPERTS_PER_GROUP)
    el = jnp.where(sel, logits, -jnp.inf)
    v1 = jnp.max(el, axis=1, keepdims=True)
    i1 = jnp.min(jnp.where(el == v1, lane_f, big), axis=1, keepdims=True)
    el2 = jnp.where(lane_f == i1, -jnp.inf, el)
    v2 = jnp.max(el2, axis=1, keepdims=True)
    i2 = jnp.min(jnp.where(el2 == v2, lane_f, big), axis=1, keepdims=True)
    t = jnp.exp(v2 - v1)
    den = 1.0 + t
    w_a = g_w * (1.0 / den)
    w_b = g_w * (t / den)
    eid_ref[0] = jnp.where(lane == 0, i1 - N_EGROUPS, jnp.where(lane == 1, i2 - N_EGROUPS, 0.0)).astype(jnp.int32)
    wts_ref[0] = jnp.where(lane == 0, w_a, jnp.where(lane == 1, w_b, 0.0))


def _out_proj(attn, ssm, x, mod, wa, ws, g2, wr_hi, wr_lo, br, tm):
    B, S, _ = x.shape
    row = lambda b, i: (b, i, 0)
    const = lambda b, i: (0, 0)
    return pl.pallas_call(
        _outproj_kernel,
        grid=(B, S // tm),
        in_specs=[pl.BlockSpec((1, tm, D_ATTN), row),
                  pl.BlockSpec((1, tm, D_SSM), row),
                  pl.BlockSpec((1, tm, D_MODEL), row),
                  pl.BlockSpec((1, N_MOD, D_MODEL), lambda b, i: (b, 0, 0)),
                  pl.BlockSpec((D_ATTN, D_MODEL), const),
                  pl.BlockSpec((D_SSM, D_MODEL), const),
                  pl.BlockSpec((1, D_MODEL), const),
                  pl.BlockSpec((D_MODEL, LANES), const),
                  pl.BlockSpec((D_MODEL, LANES), const),
                  pl.BlockSpec((1, LANES), const)],
        out_specs=[pl.BlockSpec((1, tm, D_MODEL), row),
                   pl.BlockSpec((1, tm * ROW_TILES, LANES), row),
                   pl.BlockSpec((1, tm, LANES), row), pl.BlockSpec((1, tm, LANES), row)],
        out_shape=[jax.ShapeDtypeStruct((B, S, D_MODEL), F32),
                   jax.ShapeDtypeStruct((B, S * ROW_TILES, LANES), F32),
                   jax.ShapeDtypeStruct((B, S, LANES), jnp.int32), jax.ShapeDtypeStruct((B, S, LANES), F32)],
        compiler_params=_params(("parallel", "parallel")),
        name="out_proj",
    )(attn, ssm, x, mod, wa, ws, g2, wr_hi, wr_lo, br)


def _rank_kernel(eid_ref, rank_ref, cnt_ref, carry_ref, *, tt):
    @pl.when(pl.program_id(0) == 0)
    def _():
        carry_ref[...] = jnp.zeros(carry_ref.shape, F32)

    eid = eid_ref[...]
    lane = lax.broadcasted_iota(jnp.int32, eid.shape, 1)
    r_i = lax.broadcasted_iota(jnp.int32, (tt, tt), 0)
    c_i = lax.broadcasted_iota(jnp.int32, (tt, tt), 1)
    before = (c_i < r_i).astype(BF16)
    carry = carry_ref[0:1, :]
    ranks = []
    for k in range(2):
        oh = lane == eid[:, k:k + 1]
        oh_f = oh.astype(F32)
        pref = jnp.dot(before, oh.astype(BF16), preferred_element_type=F32) + carry
        ranks.append(jnp.sum(oh_f * pref, axis=1, keepdims=True))
        carry = carry + jnp.sum(oh_f, axis=0, keepdims=True)
    carry_ref[0:1, :] = carry
    rank_ref[...] = jnp.where(lane == 0, ranks[0], jnp.where(lane == 1, ranks[1], 0.0)).astype(jnp.int32)
    cnt_ref[...] = jnp.broadcast_to(carry, cnt_ref.shape).astype(jnp.int32)


def _rank(eid, tt):
    T = eid.shape[0]
    return pl.pallas_call(
        functools.partial(_rank_kernel, tt=tt),
        grid=(T // tt,),
        in_specs=[pl.BlockSpec((tt, LANES), lambda i: (i, 0))],
        out_specs=[pl.BlockSpec((tt, LANES), lambda i: (i, 0)),
                   pl.BlockSpec((SUBLANES, LANES), lambda i: (0, 0))],
        out_shape=[jax.ShapeDtypeStruct((T, LANES), jnp.int32),
                   jax.ShapeDtypeStruct((SUBLANES, LANES), jnp.int32)],
        scratch_shapes=[pltpu.VMEM((SUBLANES, LANES), F32)],
        compiler_params=_params(("arbitrary",)),
        name="rank",
    )(eid)


def _row_copy(src, dst, sem):
    return pltpu.make_async_copy(src, dst, sem)


def _dispatch_kernel(dest_hbm, n2_ref, xin_in, xin_hbm, idx_ref, idx_sem, row_sem):
    del xin_in
    i = pl.program_id(0)
    idx_cp = pltpu.make_async_copy(dest_hbm.at[pl.ds(pl.multiple_of(i * SMEM_IDX, SMEM_IDX), SMEM_IDX)],
                                   idx_ref, idx_sem)
    idx_cp.start()
    idx_cp.wait()

    def issue(t, carry):
        for k in range(2):
            _row_copy(n2_ref.at[t], xin_hbm.at[idx_ref[2 * t + k]], row_sem).start()
        return carry

    lax.fori_loop(0, TOK_TILE, issue, 0, unroll=8)

    def drain(t, carry):
        for k in range(2):
            _row_copy(n2_ref.at[0], xin_hbm.at[0], row_sem).wait()
        return carry

    lax.fori_loop(0, TOK_TILE, drain, 0, unroll=8)


def _dispatch(dest, n2_rows, xin_zero):
    T = n2_rows.shape[0]
    any_spec = pl.BlockSpec(memory_space=pl.ANY)
    return pl.pallas_call(
        _dispatch_kernel,
        grid=(T // TOK_TILE,),
        in_specs=[any_spec, pl.BlockSpec((TOK_TILE, ROW_TILES, LANES), lambda i: (i, 0, 0)), any_spec],
        out_specs=any_spec,
        out_shape=jax.ShapeDtypeStruct(xin_zero.shape, F32),
        scratch_shapes=[pltpu.SMEM((SMEM_IDX,), jnp.int32), pltpu.SemaphoreType.DMA, pltpu.SemaphoreType.DMA],
        input_output_aliases={2: 0},
        compiler_params=pltpu.CompilerParams(dimension_semantics=("arbitrary",), has_side_effects=True),
        name="dispatch",
    )(dest, n2_rows, xin_zero)


def _moe_kernel(beid_ref, nused_ref, xin_ref, w1_ref, w3_ref, w2_ref, y_ref):
    del beid_ref
    i = pl.program_id(0)

    @pl.when(i < nused_ref[0])
    def _():
        x = jnp.concatenate([xin_ref[pl.ds(s, MOE_ROWS, stride=ROW_TILES), :] for s in range(ROW_TILES)],
                            axis=1).astype(BF16)
        h1 = jnp.dot(x, w1_ref[0], preferred_element_type=F32)
        h3 = jnp.dot(x, w3_ref[0], preferred_element_type=F32)
        y = jnp.dot((_silu(h1) * h3).astype(BF16), w2_ref[0], preferred_element_type=F32)
        for s in range(ROW_TILES):
            y_ref[pl.ds(s, MOE_ROWS, stride=ROW_TILES), :] = y[:, s * LANES:(s + 1) * LANES]

    @pl.when(i >= nused_ref[0])
    def _():
        y_ref[...] = jnp.zeros(y_ref.shape, F32)


def _moe(blk_eid, n_used, xin, w1, w3, w2):
    R = xin.shape[0] // ROW_TILES
    blk = pl.BlockSpec((MOE_ROWS * ROW_TILES, LANES), lambda i, be, nu: (i, 0))
    grid_spec = pltpu.PrefetchScalarGridSpec(
        num_scalar_prefetch=2,
        grid=(R // MOE_ROWS,),
        in_specs=[blk,
                  pl.BlockSpec((1, D_MODEL, D_EXPERT), lambda i, be, nu: (be[i], 0, 0)),
                  pl.BlockSpec((1, D_MODEL, D_EXPERT), lambda i, be, nu: (be[i], 0, 0)),
                  pl.BlockSpec((1, D_EXPERT, D_MODEL), lambda i, be, nu: (be[i], 0, 0))],
        out_specs=blk,
    )
    return pl.pallas_call(
        _moe_kernel,
        grid_spec=grid_spec,
        out_shape=jax.ShapeDtypeStruct(xin.shape, F32),
        compiler_params=_params(("arbitrary",)),
        name="moe",
    )(blk_eid, n_used, xin, w1, w3, w2)


def _combine_kernel(dest_hbm, y_hbm, h_ref, wts_ref, mod_ref, fg_ref, o_ref, idx_ref, buf_ref, idx_sem, row_sem,
                    *, tiles_per_seq, n_steps):
    n = pl.program_id(0) * tiles_per_seq + pl.program_id(1)

    def idx_copy(step, slot):
        return pltpu.make_async_copy(
            dest_hbm.at[pl.ds(pl.multiple_of(step * SMEM_IDX, SMEM_IDX), SMEM_IDX)], idx_ref.at[slot],
            idx_sem.at[slot])

    def gather(slot):
        def issue(t, carry):
            for k in range(2):
                dst = buf_ref.at[slot, k, pl.ds(pl.multiple_of(t * ROW_TILES, ROW_TILES), ROW_TILES)]
                _row_copy(y_hbm.at[idx_ref[slot, 2 * t + k]], dst, row_sem.at[slot]).start()
            return carry

        lax.fori_loop(0, TOK_TILE, issue, 0, unroll=8)

    cur = n % 2
    nxt = 1 - cur

    @pl.when(n == 0)
    def _():
        idx_copy(0, 0).start()
        idx_copy(0, 0).wait()
        gather(0)
        if n_steps > 1:
            idx_copy(1, 1).start()

    @pl.when(n + 1 < n_steps)
    def _():
        idx_copy(n + 1, nxt).wait()
        gather(nxt)

    @pl.when(n + 2 < n_steps)
    def _():
        idx_copy(n + 2, cur).start()

    def drain(t, carry):
        for k in range(2):
            _row_copy(y_hbm.at[0], buf_ref.at[cur, k, pl.ds(0, ROW_TILES)], row_sem.at[cur]).wait()
        return carry

    lax.fori_loop(0, TOK_TILE, drain, 0, unroll=8)

    wts = wts_ref[0]
    ffn = jnp.zeros((TOK_TILE, D_MODEL), F32)
    for k in range(2):
        yk = jnp.concatenate([buf_ref[cur, k, pl.ds(s, TOK_TILE, stride=ROW_TILES), :] for s in range(ROW_TILES)],
                             axis=1)
        ffn = ffn + wts[:, k:k + 1] * yk
    h = h_ref[0] + mod_ref[0, 5:6, :] * ffn
    o_ref[0] = h * lax.rsqrt(jnp.mean(h * h, axis=-1, keepdims=True) + EPS) * fg_ref[...]


def _combine(dest, y_rows, h, wts, mod, fg):
    B, S, _ = h.shape
    tps = S // TOK_TILE
    row = lambda b, i: (b, i, 0)
    any_spec = pl.BlockSpec(memory_space=pl.ANY)
    return pl.pallas_call(
        functools.partial(_combine_kernel, tiles_per_seq=tps, n_steps=B * tps),
        grid=(B, tps),
        in_specs=[any_spec, any_spec,
                  pl.BlockSpec((1, TOK_TILE, D_MODEL), row),
                  pl.BlockSpec((1, TOK_TILE, LANES), row),
                  pl.BlockSpec((1, N_MOD, D_MODEL), lambda b, i: (b, 0, 0)),
                  pl.BlockSpec((1, D_MODEL), lambda b, i: (0, 0))],
        out_specs=pl.BlockSpec((1, TOK_TILE, D_MODEL), row),
        out_shape=jax.ShapeDtypeStruct((B, S, D_MODEL), F32),
        scratch_shapes=[pltpu.SMEM((2, SMEM_IDX), jnp.int32),
                        pltpu.VMEM((2, 2, TOK_TILE * ROW_TILES, LANES), F32),
                        pltpu.SemaphoreType.DMA((2,)), pltpu.SemaphoreType.DMA((2,))],
        compiler_params=_params(("arbitrary", "arbitrary")),
        name="combine",
    )(dest, y_rows, h, wts, mod, fg)


def _rope_tables(S):
    half = HEAD_DIM // 4
    inv = ROPE_THETA ** (-jnp.arange(half, dtype=F32) / half)
    rows = S // GRID_W

    def axis_tables(n):
        ang = jnp.arange(n, dtype=F32)[:, None] * inv[None, :]
        c, s = jnp.cos(ang), jnp.sin(ang)
        return jnp.concatenate([c, c], axis=1), jnp.concatenate([-s, s], axis=1)

    def per_token(r, c):
        r = jnp.broadcast_to(r[:, None, :], (rows, GRID_W, 2 * half))
        c = jnp.broadcast_to(c[None, :, :], (rows, GRID_W, 2 * half))
        return jnp.tile(jnp.concatenate([r, c], axis=-1).reshape(S, HEAD_DIM), (1, LANES // HEAD_DIM))

    (rc, rs), (cc, cs) = axis_tables(rows), axis_tables(GRID_W)
    return per_token(rc, cc), per_token(rs, cs)


def _moe_ffn(n2, h1, eid, wts, mod, w1, w3, w2, final_g):
    B, S, _ = h1.shape
    T = B * S
    n_blocks = (2 * T + MOE_ROWS - 1) // MOE_ROWS + N_EXPERTS
    R = n_blocks * MOE_ROWS
    eid2 = eid.reshape(T, LANES)
    rank, counts = _rank(eid2, 512)
    counts = counts[0, :N_EXPERTS]
    padded = (counts + MOE_ROWS - 1) // MOE_ROWS * MOE_ROWS
    pad_ends = jnp.cumsum(padded)
    pad_starts = pad_ends - padded
    e2 = eid2[:, :2]
    onehot = e2[:, :, None] == jnp.arange(N_EXPERTS, dtype=jnp.int32)[None, None, :]
    dest = (jnp.sum(jnp.where(onehot, pad_starts[None, None, :], 0), axis=-1) + rank[:, :2]).reshape(2 * T)
    blk_start = jnp.arange(n_blocks, dtype=jnp.int32) * MOE_ROWS
    blk_eid = jnp.minimum(jnp.sum(pad_ends[None, :] <= blk_start[:, None], axis=1), N_EXPERTS - 1).astype(jnp.int32)
    n_used = (pad_ends[-1:] // MOE_ROWS).astype(jnp.int32)
    xin = _dispatch(dest, n2.reshape(T, ROW_TILES, LANES), jnp.zeros((R, ROW_TILES, LANES), F32))
    y_rows = _moe(blk_eid, n_used, xin.reshape(R * ROW_TILES, LANES), w1, w3, w2)
    return _combine(dest, y_rows.reshape(R, ROW_TILES, LANES), h1, wts, mod, final_g)


def _trunk(x, mod, p):
    B, S, _ = x.shape
    cos, sin = _rope_tables(S)
    q, k, v, z, xbc, dtr = _in_proj(x, mod, p["g1"], p["w_in"], p["qg"], p["kg"], cos, sin, p["seg"], tm=512)
    attn = _attention(q, k, v, p["attn_g"], tq=min(256, S), tk=min(1024, S))
    xact = _conv(xbc, p["conv_w"], p["conv_b"], tr=512)
    y_b = _ssd(xact, dtr, p["dt_bias"], p["a_log"], None, rev=True, rows=512)
    ssm = _ssd(xact, dtr, p["dt_bias"], p["a_log"], (y_b, z, p["d_skip"], p["ssm_g"]), rev=False, rows=512)
    h1, n2, eid, wts = _out_proj(attn, ssm, x, mod, p["w_out_a"], p["w_out_s"], p["g2"], p["w_r_hi"], p["w_r_lo"],
                                  p["b_r"], tm=512)
    return _moe_ffn(n2, h1, eid, wts, mod, p["w1"], p["w3"], p["w2"], p["final_g"])


def kernel(x_prompt, x_sample, c_prompt, c_sample, w_ada, b_ada, norm1_g, w_in, q_norm_g, k_norm_g, attn_out_g,
           conv_w, conv_b, a_log, dt_bias, d_skip, ssm_norm_g, w_out, norm2_g, w_rg, b_rg, w_re, b_re, w1, w3, w2,
           final_g):
    l = 0
    Bp, Bs = c_prompt.shape[0], c_sample.shape[0]
    rows = (Bp + Bs + SUBLANES - 1) // SUBLANES * SUBLANES
    c_all = jnp.zeros((rows, D_MODEL), F32).at[:Bp].set(c_prompt).at[Bp:Bp + Bs].set(c_sample)
    mod = _modulation(c_all, w_ada[l], b_ada[l].reshape(1, -1))
    mod_p = mod[:Bp].reshape(Bp, N_MOD, D_MODEL)
    mod_s = mod[Bp:Bp + Bs].reshape(Bs, N_MOD, D_MODEL)

    def lane_pad(a):
        return jnp.pad(a.reshape(1, -1), ((0, 0), (0, LANES - a.size)))

    head_id = jnp.arange(LANES) // HEAD_DIM
    w_r = jnp.concatenate([w_rg[l]] + [w_re[l, g] for g in range(N_EGROUPS)], axis=1)
    b_r = jnp.concatenate([b_rg[l]] + [b_re[l, g] for g in range(N_EGROUPS)])
    w_r = jnp.pad(w_r, ((0, 0), (0, LANES - w_r.shape[1])))
    w_r_hi = w_r.astype(BF16)
    p = {
        "g1": norm1_g[l].reshape(1, -1),
        "w_in": jnp.pad(w_in[l], ((0, 0), (0, D_IN_PAD - D_IN_PROJ))).astype(BF16),
        "qg": jnp.tile(q_norm_g[l] * (HEAD_DIM ** -0.5), N_HEADS).reshape(1, -1),
        "kg": jnp.tile(k_norm_g[l], N_KV_HEADS).reshape(1, -1),
        "seg": (head_id[:, None] == head_id[None, :]).astype(BF16),
        "attn_g": attn_out_g[l].reshape(1, -1),
        "conv_w": conv_w[l].reshape(CONV_K, D_XBC),
        "conv_b": conv_b[l].reshape(1, -1),
        "dt_bias": lane_pad(dt_bias[l]),
        "a_log": lane_pad(a_log[l]),
        "d_skip": jnp.repeat(d_skip[l], SSM_HEAD_DIM).reshape(1, -1),
        "ssm_g": ssm_norm_g[l].reshape(1, -1),
        "w_out_a": w_out[l, :D_ATTN].astype(BF16),
        "w_out_s": w_out[l, D_ATTN:].astype(BF16),
        "g2": norm2_g[l].reshape(1, -1),
        "w_r_hi": w_r_hi,
        "w_r_lo": (w_r - w_r_hi.astype(F32)).astype(BF16),
        "b_r": lane_pad(b_r),
        "w1": w1[l].astype(BF16),
        "w3": w3[l].astype(BF16),
        "w2": w2[l].astype(BF16),
        "final_g": final_g.reshape(1, -1),
    }
    return _trunk(x_prompt, mod_p, p), _trunk(x_sample, mod_s, p)
```

```python
import functools

import jax
import jax.numpy as jnp
from jax import lax
from jax.experimental import pallas as pl
from jax.experimental.pallas import tpu as pltpu

F32 = jnp.float32
BF16 = jnp.bfloat16
HIGHEST = lax.Precision.HIGHEST

D_MODEL = 1024
GRID_W = 64
EPS = 1e-6
N_HEADS = 8
N_KV_HEADS = 2
HEAD_DIM = 64
D_ATTN = N_HEADS * HEAD_DIM
D_KV = N_KV_HEADS * HEAD_DIM
ROPE_THETA = 10000.0
SSM_HEADS = 8
SSM_HEAD_DIM = 64
D_SSM = SSM_HEADS * SSM_HEAD_DIM
SSM_STATE = 64
SSM_GROUPS = 2
CONV_K = 5
CHUNK = 128
D_XBC = D_SSM + 2 * SSM_GROUPS * SSM_STATE
D_IN_PROJ = D_ATTN + 2 * D_KV + 2 * D_SSM + 2 * SSM_GROUPS * SSM_STATE + 2 * SSM_HEADS
N_EGROUPS = 4
EXPERTS_PER_GROUP = 8
N_EXPERTS = N_EGROUPS * EXPERTS_PER_GROUP
D_EXPERT = 512
N_MOD = 6

LANES = 128
SUBLANES = 8
ROW_TILES = D_MODEL // LANES
D_IN_PAD = (D_IN_PROJ + LANES - 1) // LANES * LANES
SMEM_IDX = 1024
TOK_TILE = SMEM_IDX // 2
MOE_ROWS = 256
VMEM_LIMIT = 48 * 1024 * 1024

_Q0, _K0, _V0, _Z0, _X0, _DT0 = 0, D_ATTN, D_ATTN + D_KV, D_ATTN + 2 * D_KV, D_ATTN + 2 * D_KV + D_SSM, \
    D_ATTN + 2 * D_KV + D_SSM + D_XBC


def _params(sem):
    return pltpu.CompilerParams(dimension_semantics=sem, vmem_limit_bytes=VMEM_LIMIT)


def _silu(x):
    return x * jax.nn.sigmoid(x)


def _mod_kernel(c_ref, w_ref, b_ref, o_ref):
    o_ref[...] = jnp.dot(_silu(c_ref[...]), w_ref[...], precision=HIGHEST,
                         preferred_element_type=F32) + b_ref[...]


def _modulation(c, w, b):
    rows, n = c.shape[0], w.shape[1]
    tn = 1024
    return pl.pallas_call(
        _mod_kernel,
        grid=(n // tn,),
        in_specs=[pl.BlockSpec((rows, D_MODEL), lambda j: (0, 0)),
                  pl.BlockSpec((D_MODEL, tn), lambda j: (0, j)),
                  pl.BlockSpec((1, tn), lambda j: (0, j))],
        out_specs=pl.BlockSpec((rows, tn), lambda j: (0, j)),
        out_shape=jax.ShapeDtypeStruct((rows, n), F32),
        compiler_params=_params(("arbitrary",)),
        name="mod",
    )(c, w, b)


def _inproj_kernel(x_ref, mod_ref, g1_ref, w_ref, qg_ref, kg_ref, cos_ref, sin_ref, seg_ref,
                   q_ref, k_ref, v_ref, z_ref, xbc_ref, dt_ref):
    x = x_ref[0]
    ms = jnp.mean(x * x, axis=-1, keepdims=True)
    xn = x * lax.rsqrt(ms + EPS) * g1_ref[...]
    n1 = xn * (1.0 + mod_ref[0, 1:2, :]) + mod_ref[0, 0:1, :]
    proj = jnp.dot(n1.astype(BF16), w_ref[...], preferred_element_type=F32)

    cos = cos_ref[...]
    sin = sin_ref[...]
    seg = seg_ref[...]
    lane = lax.broadcasted_iota(jnp.int32, cos.shape, 1)
    low_half = (lane % 32) < 16

    def norm_rope(xc, gain):
        sq = xc * xc
        hi = sq.astype(BF16)
        lo = (sq - hi.astype(F32)).astype(BF16)
        ss = (jnp.dot(hi, seg, preferred_element_type=F32)
              + jnp.dot(lo, seg, preferred_element_type=F32))
        xg = xc * lax.rsqrt(ss * (1.0 / HEAD_DIM) + EPS) * gain
        rot = jnp.where(low_half, pltpu.roll(xg, LANES - 16, 1), pltpu.roll(xg, 16, 1))
        return xg * cos + rot * sin

    for c in range(D_ATTN // LANES):
        sl = slice(c * LANES, (c + 1) * LANES)
        q_ref[0, :, sl] = norm_rope(proj[:, _Q0 + c * LANES:_Q0 + (c + 1) * LANES], qg_ref[:, sl]).astype(BF16)
    k_ref[0] = norm_rope(proj[:, _K0:_V0], kg_ref[...]).astype(BF16)
    v = proj[:, _V0:_Z0]
    for j in range(N_KV_HEADS):
        mine = (lane >= HEAD_DIM * j) & (lane < HEAD_DIM * (j + 1))
        v_ref[0, j] = jnp.where(mine, v, 1.0).astype(BF16)
    z_ref[0] = proj[:, _Z0:_X0]
    xbc_ref[0] = proj[:, _X0:_DT0]
    dt_ref[0] = proj[:, _DT0:D_IN_PAD]


def _in_proj(x, mod, g1, w_in, qg, kg, cos, sin, seg, tm):
    B, S, _ = x.shape
    row = lambda b, i: (b, i, 0)
    const = lambda b, i: (0, 0)
    outs = [(D_ATTN, BF16), (D_KV, BF16), None, (D_SSM, F32), (D_XBC, F32), (LANES, F32)]
    out_specs = [pl.BlockSpec((1, tm, o[0]), row) if o else
                 pl.BlockSpec((1, N_KV_HEADS, tm, D_KV), lambda b, i: (b, 0, i, 0)) for o in outs]
    out_shape = [jax.ShapeDtypeStruct((B, S, o[0]), o[1]) if o else
                 jax.ShapeDtypeStruct((B, N_KV_HEADS, S, D_KV), BF16) for o in outs]
    return pl.pallas_call(
        _inproj_kernel,
        grid=(B, S // tm),
        in_specs=[pl.BlockSpec((1, tm, D_MODEL), row),
                  pl.BlockSpec((1, N_MOD, D_MODEL), lambda b, i: (b, 0, 0)),
                  pl.BlockSpec((1, D_MODEL), const),
                  pl.BlockSpec((D_MODEL, D_IN_PAD), const),
                  pl.BlockSpec((1, D_ATTN), const),
                  pl.BlockSpec((1, D_KV), const),
                  pl.BlockSpec((tm, LANES), lambda b, i: (i, 0)),
                  pl.BlockSpec((tm, LANES), lambda b, i: (i, 0)),
                  pl.BlockSpec((LANES, LANES), const)],
        out_specs=out_specs,
        out_shape=out_shape,
        compiler_params=_params(("parallel", "parallel")),
        name="in_proj",
    )(x, mod, g1, w_in, qg, kg, cos, sin, seg)


def _attn_kernel(q_ref, k_ref, v_ref, g_ref, o_ref, qs_ref, m_ref, acc_ref, *, tq, tk, nk):
    G = N_HEADS // N_KV_HEADS
    lane = lax.broadcasted_iota(jnp.int32, (tq, LANES), 1)
    qf = q_ref[0].astype(F32)
    for j in range(N_KV_HEADS):
        pieces = []
        for g in range(G):
            h = G * j + g
            chunk = qf[:, (h // 2) * LANES:(h // 2 + 1) * LANES]
            if h % 2 != j:
                chunk = pltpu.roll(chunk, HEAD_DIM, 1)
            keep = (lane >= HEAD_DIM * j) & (lane < HEAD_DIM * (j + 1))
            pieces.append(jnp.where(keep, chunk, 0.0))
        qs_ref[j] = jnp.concatenate(pieces, axis=0).astype(BF16)
    m_ref[...] = jnp.full(m_ref.shape, -jnp.inf, F32)
    acc_ref[...] = jnp.zeros(acc_ref.shape, F32)

    def body(c, carry):
        off = pl.multiple_of(c * tk, tk)
        kc = k_ref[0, pl.ds(off, tk), :]
        for j in range(N_KV_HEADS):
            s = lax.dot_general(qs_ref[j], kc, (((1,), (1,)), ((), ())), preferred_element_type=F32)
            m_prev = m_ref[j]
            m_new = jnp.maximum(m_prev, jnp.max(s, axis=1, keepdims=True))
            alpha = jnp.exp(m_prev - m_new)
            p = jnp.exp(s - jnp.concatenate([m_new] * (tk // LANES), axis=1))
            acc_ref[j] = alpha * acc_ref[j] + jnp.dot(p.astype(BF16), v_ref[0, j, pl.ds(off, tk), :],
                                                      preferred_element_type=F32)
            m_ref[j] = m_new
        return carry

    lax.fori_loop(0, nk, body, 0, unroll=4)

    o = [acc_ref[j] / pltpu.roll(acc_ref[j], HEAD_DIM, 1) for j in range(N_KV_HEADS)]

    def head(h, half):
        j, g = divmod(h, G)
        piece = o[j][g * tq:(g + 1) * tq, :]
        return pltpu.roll(piece, HEAD_DIM, 1) if j != half else piece

    of = jnp.concatenate(
        [jnp.where(lane < HEAD_DIM, head(2 * c, 0), head(2 * c + 1, 1)) for c in range(N_HEADS // 2)], axis=1)
    ms = jnp.mean(of * of, axis=-1, keepdims=True)
    o_ref[0] = (of * lax.rsqrt(ms + EPS) * g_ref[...]).astype(BF16)


def _attention(q, k, v, g, tq, tk):
    B, S, _ = q.shape
    G = N_HEADS // N_KV_HEADS
    kern = functools.partial(_attn_kernel, tq=tq, tk=tk, nk=S // tk)
    return pl.pallas_call(
        kern,
        grid=(B, S // tq),
        in_specs=[pl.BlockSpec((1, tq, D_ATTN), lambda b, i: (b, i, 0)),
                  pl.BlockSpec((1, S, D_KV), lambda b, i: (b, 0, 0)),
                  pl.BlockSpec((1, N_KV_HEADS, S, D_KV), lambda b, i: (b, 0, 0, 0)),
                  pl.BlockSpec((1, D_ATTN), lambda b, i: (0, 0))],
        out_specs=pl.BlockSpec((1, tq, D_ATTN), lambda b, i: (b, i, 0)),
        out_shape=jax.ShapeDtypeStruct((B, S, D_ATTN), BF16),
        scratch_shapes=[pltpu.VMEM((N_KV_HEADS, G * tq, LANES), BF16),
                        pltpu.VMEM((N_KV_HEADS, G * tq, LANES), F32),
                        pltpu.VMEM((N_KV_HEADS, G * tq, LANES), F32)],
        compiler_params=_params(("parallel", "parallel")),
        name="attn",
    )(q, k, v, g)


def _conv_kernel(xp_ref, xc_ref, xn_ref, w_ref, b_ref, o_ref, ext_ref, *, tr, nblk):
    i = pl.program_id(1)
    pad = CONV_K // 2
    ext_ref[0:SUBLANES, :] = jnp.where(i > 0, xp_ref[0], 0.0)
    ext_ref[SUBLANES:SUBLANES + tr, :] = xc_ref[0]
    ext_ref[SUBLANES + tr:, :] = jnp.where(i < nblk - 1, xn_ref[0], 0.0)
    acc = jnp.zeros((tr, D_XBC), F32) + b_ref[...]
    for kk in range(CONV_K):
        acc = acc + w_ref[kk:kk + 1, :] * ext_ref[SUBLANES - pad + kk:SUBLANES - pad + kk + tr, :]
    o_ref[0] = _silu(acc)


def _conv(xbc, w, b, tr):
    B, S, _ = xbc.shape
    nblk = S // tr
    per = tr // SUBLANES
    kern = functools.partial(_conv_kernel, tr=tr, nblk=nblk)
    return pl.pallas_call(
        kern,
        grid=(B, nblk),
        in_specs=[pl.BlockSpec((1, SUBLANES, D_XBC), lambda b, i: (b, jnp.maximum(i * per - 1, 0), 0)),
                  pl.BlockSpec((1, tr, D_XBC), lambda b, i: (b, i, 0)),
                  pl.BlockSpec((1, SUBLANES, D_XBC),
                               lambda b, i: (b, jnp.minimum((i + 1) * per, S // SUBLANES - 1), 0)),
                  pl.BlockSpec((CONV_K, D_XBC), lambda b, i: (0, 0)),
                  pl.BlockSpec((1, D_XBC), lambda b, i: (0, 0))],
        out_specs=pl.BlockSpec((1, tr, D_XBC), lambda b, i: (b, i, 0)),
        out_shape=jax.ShapeDtypeStruct((B, S, D_XBC), F32),
        scratch_shapes=[pltpu.VMEM((tr + 2 * SUBLANES, D_XBC), F32)],
        compiler_params=_params(("parallel", "parallel")),
        name="conv",
    )(xbc, xbc, xbc, w, b)


def _ssd_kernel(*refs, rev, rows):
    if rev:
        xbc_ref, dt_ref, dtb_ref, alog_ref, o_ref, st_ref = refs
    else:
        xbc_ref, dt_ref, dtb_ref, alog_ref, yb_ref, z_ref, dsk_ref, ng_ref, o_ref, st_ref = refs
    L = CHUNK
    d = 1 if rev else 0
    edge = 0 if rev else L - 1
    nchunk = rows // L

    @pl.when(pl.program_id(1) == 0)
    def _():
        st_ref[...] = jnp.zeros(st_ref.shape, F32)

    r_i = lax.broadcasted_iota(jnp.int32, (L, L), 0)
    c_i = lax.broadcasted_iota(jnp.int32, (L, L), 1)
    causal = (c_i >= r_i) if rev else (r_i >= c_i)
    tri = causal.astype(F32)
    low = c_i < SSM_HEAD_DIM
    a_row = -jnp.exp(alog_ref[...])

    def chunk(ci, carry):
        cc = (nchunk - 1 - ci) if rev else ci
        off = pl.multiple_of(cc * L, L)
        xbc = xbc_ref[0, pl.ds(off, L), :]
        xs = xbc[:, :D_SSM]
        bm = xbc[:, D_SSM:D_SSM + LANES]
        cm = xbc[:, D_SSM + LANES:]
        x_dt = dt_ref[0, pl.ds(off, L), :] + dtb_ref[...]
        dt_all = jnp.maximum(x_dt, 0.0) + jnp.log(1.0 + jnp.exp(-jnp.abs(x_dt)))
        a_all = dt_all * a_row
        acs = jnp.dot(tri, a_all, precision=HIGHEST, preferred_element_type=F32)
        acs_t = acs.T
        bm_b = bm.astype(BF16)
        bm_t = bm.T.astype(BF16)
        cg = [jnp.where(low, cm, 0.0).astype(BF16), jnp.where(low, 0.0, cm).astype(BF16)]
        cb = [lax.dot_general(cg[g], bm_b, (((1,), (1,)), ((), ())), preferred_element_type=F32)
              for g in range(SSM_GROUPS)]
        ys = []
        for c in range(SSM_HEADS // 2):
            g = (2 * c) // (SSM_HEADS // SSM_GROUPS)
            ja, jb = d * SSM_HEADS + 2 * c, d * SSM_HEADS + 2 * c + 1
            sl = slice(c * LANES, (c + 1) * LANES)
            dt_pair = jnp.where(low, dt_all[:, ja:ja + 1], dt_all[:, jb:jb + 1])
            acs_pair = jnp.where(low, acs[:, ja:ja + 1], acs[:, jb:jb + 1])
            xdt = xs[:, sl] * dt_pair
            xdt_b = xdt.astype(BF16)
            y_heads = []
            for jh in (ja, jb):
                seg = acs[:, jh:jh + 1] - acs_t[jh:jh + 1, :]
                lmat = jnp.exp(jnp.where(causal, seg, -jnp.inf))
                y_heads.append(jnp.dot((cb[g] * lmat).astype(BF16), xdt_b, preferred_element_type=F32))
            y = jnp.where(low, y_heads[0], y_heads[1])
            st = st_ref[:, sl]
            y = y + jnp.dot(cg[g], st.astype(BF16), preferred_element_type=F32) * jnp.exp(acs_pair)
            ys.append(y)
            edge_row = acs_pair[edge:edge + 1, :]
            xdec = (xdt * jnp.exp(edge_row - acs_pair)).astype(BF16)
            st_ref[:, sl] = jnp.exp(edge_row) * st + jnp.dot(bm_t, xdec, preferred_element_type=F32)
        y = jnp.concatenate(ys, axis=1)
        if rev:
            o_ref[0, pl.ds(off, L), :] = y
        else:
            y = y + yb_ref[0, pl.ds(off, L), :] + dsk_ref[...] * xs
            y = y * _silu(z_ref[0, pl.ds(off, L), :])
            gw = D_SSM // SSM_GROUPS
            outs = []
            for g in range(SSM_GROUPS):
                yg = y[:, g * gw:(g + 1) * gw]
                outs.append(yg * lax.rsqrt(jnp.mean(yg * yg, axis=-1, keepdims=True) + EPS))
            o_ref[0, pl.ds(off, L), :] = (jnp.concatenate(outs, axis=1) * ng_ref[...]).astype(BF16)
        return carry

    lax.fori_loop(0, nchunk, chunk, 0, unroll=2)


def _ssd(xbc, dt, dtb, alog, extra, rev, rows):
    B, S, _ = xbc.shape
    nblk = S // rows
    blk = (lambda b, i: (b, nblk - 1 - i, 0)) if rev else (lambda b, i: (b, i, 0))
    const = lambda b, i: (0, 0)
    in_specs = [pl.BlockSpec((1, rows, D_XBC), blk),
                pl.BlockSpec((1, rows, LANES), blk),
                pl.BlockSpec((1, LANES), const),
                pl.BlockSpec((1, LANES), const)]
    args = [xbc, dt, dtb, alog]
    if not rev:
        yb, z, dsk, ng = extra
        in_specs += [pl.BlockSpec((1, rows, D_SSM), blk), pl.BlockSpec((1, rows, D_SSM), blk),
                     pl.BlockSpec((1, D_SSM), const), pl.BlockSpec((1, D_SSM), const)]
        args += [yb, z, dsk, ng]
    return pl.pallas_call(
        functools.partial(_ssd_kernel, rev=rev, rows=rows),
        grid=(B, nblk),
        in_specs=in_specs,
        out_specs=pl.BlockSpec((1, rows, D_SSM), blk),
        out_shape=jax.ShapeDtypeStruct((B, S, D_SSM), F32 if rev else BF16),
        scratch_shapes=[pltpu.VMEM((SSM_GROUPS * SSM_STATE, D_SSM), F32)],
        compiler_params=_params(("parallel", "arbitrary")),
        name="ssd_bwd" if rev else "ssd_fwd",
    )(*args)


def _outproj_kernel(a_ref, s_ref, x_ref, mod_ref, wa_ref, ws_ref, g2_ref, wrh_ref, wrl_ref, br_ref,
                    h_ref, n2_ref, eid_ref, wts_ref):
    mix = (jnp.dot(a_ref[0], wa_ref[...], preferred_element_type=F32)
           + jnp.dot(s_ref[0], ws_ref[...], preferred_element_type=F32))
    h = x_ref[0] + mod_ref[0, 2:3, :] * mix
    h_ref[0] = h
    hn = h * lax.rsqrt(jnp.mean(h * h, axis=-1, keepdims=True) + EPS) * g2_ref[...]
    n2 = hn * (1.0 + mod_ref[0, 4:5, :]) + mod_ref[0, 3:4, :]
    for s in range(ROW_TILES):
        n2_ref[0, pl.ds(s, n2.shape[0], stride=ROW_TILES), :] = n2[:, s * LANES:(s + 1) * LANES]

    n_hi = n2.astype(BF16)
    n_lo = (n2 - n_hi.astype(F32)).astype(BF16)
    logits = (jnp.dot(n_hi, wrh_ref[...], preferred_element_type=F32)
              + (jnp.dot(n_lo, wrh_ref[...], preferred_element_type=F32)
                 + jnp.dot(n_hi, wrl_ref[...], preferred_element_type=F32))) + br_ref[...]
    lane = lax.broadcasted_iota(jnp.int32, logits.shape, 1)
    lane_f = lane.astype(F32)
    big = float(LANES)
    is_g = lane < N_EGROUPS
    gl = jnp.where(is_g, logits, -jnp.inf)
    gmax = jnp.max(gl, axis=1, keepdims=True)
    gidx = jnp.min(jnp.where(gl == gmax, lane_f, big), axis=1, keepdims=True)
    g_w = 1.0 / jnp.sum(jnp.where(is_g, jnp.exp(gl - gmax), 0.0), axis=1, keepdims=True)
    e_lo = N_EGROUPS + EXPERTS_PER_GROUP * gidx
    sel = (lane_f >= e_lo) & (lane_f < e_lo + EXPERTS_PER_GROUP)
    el = jnp.where(sel, logits, -jnp.inf)
    v1 = jnp.max(el, axis=1, keepdims=True)
    i1 = jnp.min(jnp.where(el == v1, lane_f, big), axis=1, keepdims=True)
    el2 = jnp.where(lane_f == i1, -jnp.inf, el)
    v2 = jnp.max(el2, axis=1, keepdims=True)
    i2 = jnp.min(jnp.where(el2 == v2, lane_f, big), axis=1, keepdims=True)
    t = jnp.exp(v2 - v1)
    den = 1.0 + t
    w_a = g_w * (1.0 / den)
    w_b = g_w * (t / den)
    eid_ref[0] = jnp.where(lane == 0, i1 - N_EGROUPS, jnp.where(lane == 1, i2 - N_EGROUPS, 0.0)).astype(jnp.int32)
    wts_ref[0] = jnp.where(lane == 0, w_a, jnp.where(lane == 1, w_b, 0.0))


def _out_proj(attn, ssm, x, mod, wa, ws, g2, wr_hi, wr_lo, br, tm):
    B, S, _ = x.shape
    row = lambda b, i: (b, i, 0)
    const = lambda b, i: (0, 0)
    return pl.pallas_call(
        _outproj_kernel,
        grid=(B, S // tm),
        in_specs=[pl.BlockSpec((1, tm, D_ATTN), row),
                  pl.BlockSpec((1, tm, D_SSM), row),
                  pl.BlockSpec((1, tm, D_MODEL), row),
                  pl.BlockSpec((1, N_MOD, D_MODEL), lambda b, i: (b, 0, 0)),
                  pl.BlockSpec((D_ATTN, D_MODEL), const),
                  pl.BlockSpec((D_SSM, D_MODEL), const),
                  pl.BlockSpec((1, D_MODEL), const),
                  pl.BlockSpec((D_MODEL, LANES), const),
                  pl.BlockSpec((D_MODEL, LANES), const),
                  pl.BlockSpec((1, LANES), const)],
        out_specs=[pl.BlockSpec((1, tm, D_MODEL), row),
                   pl.BlockSpec((1, tm * ROW_TILES, LANES), row),
                   pl.BlockSpec((1, tm, LANES), row), pl.BlockSpec((1, tm, LANES), row)],
        out_shape=[jax.ShapeDtypeStruct((B, S, D_MODEL), F32),
                   jax.ShapeDtypeStruct((B, S * ROW_TILES, LANES), F32),
                   jax.ShapeDtypeStruct((B, S, LANES), jnp.int32), jax.ShapeDtypeStruct((B, S, LANES), F32)],
        compiler_params=_params(("parallel", "parallel")),
        name="out_proj",
    )(attn, ssm, x, mod, wa, ws, g2, wr_hi, wr_lo, br)


def _rank_kernel(eid_ref, rank_ref, cnt_ref, carry_ref, *, tt):
    @pl.when(pl.program_id(0) == 0)
    def _():
        carry_ref[...] = jnp.zeros(carry_ref.shape, F32)

    eid = eid_ref[...]
    lane = lax.broadcasted_iota(jnp.int32, eid.shape, 1)
    r_i = lax.broadcasted_iota(jnp.int32, (tt, tt), 0)
    c_i = lax.broadcasted_iota(jnp.int32, (tt, tt), 1)
    before = (c_i < r_i).astype(BF16)
    carry = carry_ref[0:1, :]
    ranks = []
    for k in range(2):
        oh = lane == eid[:, k:k + 1]
        oh_f = oh.astype(F32)
        pref = jnp.dot(before, oh.astype(BF16), preferred_element_type=F32) + carry
        ranks.append(jnp.sum(oh_f * pref, axis=1, keepdims=True))
        carry = carry + jnp.sum(oh_f, axis=0, keepdims=True)
    carry_ref[0:1, :] = carry
    rank_ref[...] = jnp.where(lane == 0, ranks[0], jnp.where(lane == 1, ranks[1], 0.0)).astype(jnp.int32)
    cnt_ref[...] = jnp.broadcast_to(carry, cnt_ref.shape).astype(jnp.int32)


def _rank(eid, tt):
    T = eid.shape[0]
    return pl.pallas_call(
        functools.partial(_rank_kernel, tt=tt),
        grid=(T // tt,),
        in_specs=[pl.BlockSpec((tt, LANES), lambda i: (i, 0))],
        out_specs=[pl.BlockSpec((tt, LANES), lambda i: (i, 0)),
                   pl.BlockSpec((SUBLANES, LANES), lambda i: (0, 0))],
        out_shape=[jax.ShapeDtypeStruct((T, LANES), jnp.int32),
                   jax.ShapeDtypeStruct((SUBLANES, LANES), jnp.int32)],
        scratch_shapes=[pltpu.VMEM((SUBLANES, LANES), F32)],
        compiler_params=_params(("arbitrary",)),
        name="rank",
    )(eid)


def _row_copy(src, dst, sem):
    return pltpu.make_async_copy(src, dst, sem)


def _dispatch_kernel(pend_ref, nused_ref, dest_hbm, n2_ref, xin_hbm, idx_ref, zero_ref, idx_sem, row_sem, zero_sem,
                     *, n_blocks):
    i = pl.program_id(0)

    @pl.when(i == 0)
    def _():
        zero_ref[...] = jnp.zeros(zero_ref.shape, F32)

        def zero_copy(blk):
            rows = pl.ds(pl.multiple_of(blk * MOE_ROWS, MOE_ROWS), MOE_ROWS)
            return pltpu.make_async_copy(zero_ref, xin_hbm.at[rows], zero_sem)

        def region_tail(fn):
            def per_expert(e, carry):
                start = jnp.where(e > 0, pend_ref[jnp.maximum(e - 1, 0)], 0)

                @pl.when(pend_ref[e] > start)
                def _():
                    fn(zero_copy(pend_ref[e] // MOE_ROWS - 1))

                return carry

            lax.fori_loop(0, N_EXPERTS, per_expert, 0)

            def per_unused(blk, carry):
                fn(zero_copy(blk))
                return carry

            lax.fori_loop(nused_ref[0], n_blocks, per_unused, 0)

        region_tail(lambda cp: cp.start())
        region_tail(lambda cp: cp.wait())

    idx_cp = pltpu.make_async_copy(dest_hbm.at[pl.ds(pl.multiple_of(i * SMEM_IDX, SMEM_IDX), SMEM_IDX)],
                                   idx_ref, idx_sem)
    idx_cp.start()
    idx_cp.wait()

    def issue(t, carry):
        for k in range(2):
            _row_copy(n2_ref.at[t], xin_hbm.at[idx_ref[2 * t + k]], row_sem).start(priority=k)
        return carry

    lax.fori_loop(0, TOK_TILE, issue, 0, unroll=8)

    def drain(t, carry):
        for k in range(2):
            _row_copy(n2_ref.at[0], xin_hbm.at[0], row_sem).wait()
        return carry

    lax.fori_loop(0, TOK_TILE, drain, 0, unroll=8)


def _dispatch(pad_ends, n_used, dest, n2_rows, n_blocks):
    T = n2_rows.shape[0]
    any_spec = pl.BlockSpec(memory_space=pl.ANY)
    grid_spec = pltpu.PrefetchScalarGridSpec(
        num_scalar_prefetch=2,
        grid=(T // TOK_TILE,),
        in_specs=[any_spec, pl.BlockSpec((TOK_TILE, ROW_TILES, LANES), lambda i, pe, nu: (i, 0, 0))],
        out_specs=any_spec,
        scratch_shapes=[pltpu.SMEM((SMEM_IDX,), jnp.int32), pltpu.VMEM((MOE_ROWS, ROW_TILES, LANES), F32),
                        pltpu.SemaphoreType.DMA, pltpu.SemaphoreType.DMA, pltpu.SemaphoreType.DMA],
    )
    return pl.pallas_call(
        functools.partial(_dispatch_kernel, n_blocks=n_blocks),
        grid_spec=grid_spec,
        out_shape=jax.ShapeDtypeStruct((n_blocks * MOE_ROWS, ROW_TILES, LANES), F32),
        compiler_params=_params(("arbitrary",)),
        name="dispatch",
    )(pad_ends, n_used, dest, n2_rows)


def _moe_kernel(beid_ref, nused_ref, xin_ref, w1_ref, w3_ref, w2_ref, y_ref, w1b_ref, w3b_ref, w2b_ref):
    i = pl.program_id(0)

    @pl.when((i == 0) | (beid_ref[i] != beid_ref[jnp.maximum(i - 1, 0)]))
    def _():
        w1b_ref[...] = w1_ref[0].astype(BF16)
        w3b_ref[...] = w3_ref[0].astype(BF16)
        w2b_ref[...] = w2_ref[0].astype(BF16)

    @pl.when(i < nused_ref[0])
    def _():
        x = jnp.concatenate([xin_ref[pl.ds(s, MOE_ROWS, stride=ROW_TILES), :] for s in range(ROW_TILES)],
                            axis=1).astype(BF16)
        h1 = jnp.dot(x, w1b_ref[...], preferred_element_type=F32)
        h3 = jnp.dot(x, w3b_ref[...], preferred_element_type=F32)
        y = jnp.dot((_silu(h1) * h3).astype(BF16), w2b_ref[...], preferred_element_type=F32)
        for s in range(ROW_TILES):
            y_ref[pl.ds(s, MOE_ROWS, stride=ROW_TILES), :] = y[:, s * LANES:(s + 1) * LANES]

    @pl.when(i >= nused_ref[0])
    def _():
        y_ref[...] = jnp.zeros(y_ref.shape, F32)


def _moe(blk_eid, n_used, xin, w1, w3, w2):
    R = xin.shape[0] // ROW_TILES
    rows = MOE_ROWS * ROW_TILES
    grid_spec = pltpu.PrefetchScalarGridSpec(
        num_scalar_prefetch=2,
        grid=(R // MOE_ROWS,),
        in_specs=[pl.BlockSpec((rows, LANES), lambda i, be, nu: (jnp.minimum(i, nu[0] - 1), 0)),
                  pl.BlockSpec((1, D_MODEL, D_EXPERT), lambda i, be, nu: (be[i], 0, 0)),
                  pl.BlockSpec((1, D_MODEL, D_EXPERT), lambda i, be, nu: (be[i], 0, 0)),
                  pl.BlockSpec((1, D_EXPERT, D_MODEL), lambda i, be, nu: (be[i], 0, 0))],
        out_specs=pl.BlockSpec((rows, LANES), lambda i, be, nu: (i, 0)),
        scratch_shapes=[pltpu.VMEM((D_MODEL, D_EXPERT), BF16), pltpu.VMEM((D_MODEL, D_EXPERT), BF16),
                        pltpu.VMEM((D_EXPERT, D_MODEL), BF16)],
    )
    return pl.pallas_call(
        _moe_kernel,
        grid_spec=grid_spec,
        out_shape=jax.ShapeDtypeStruct(xin.shape, F32),
        compiler_params=_params(("arbitrary",)),
        name="moe",
    )(blk_eid, n_used, xin, w1, w3, w2)


def _combine_kernel(dest_hbm, y_hbm, h_ref, wts_ref, mod_ref, fg_ref, o_ref, idx_ref, buf_ref, idx_sem, row_sem,
                    *, tiles_per_seq, n_steps):
    n = pl.program_id(0) * tiles_per_seq + pl.program_id(1)

    def idx_copy(step, slot):
        return pltpu.make_async_copy(
            dest_hbm.at[pl.ds(pl.multiple_of(step * SMEM_IDX, SMEM_IDX), SMEM_IDX)], idx_ref.at[slot],
            idx_sem.at[slot])

    def gather(slot):
        def issue(t, carry):
            for k in range(2):
                dst = buf_ref.at[slot, k, pl.ds(pl.multiple_of(t * ROW_TILES, ROW_TILES), ROW_TILES)]
                _row_copy(y_hbm.at[idx_ref[slot, 2 * t + k]], dst, row_sem.at[slot]).start(priority=k)
            return carry

        lax.fori_loop(0, TOK_TILE, issue, 0, unroll=8)

    cur = n % 2
    nxt = 1 - cur

    @pl.when(n == 0)
    def _():
        idx_copy(0, 0).start()
        idx_copy(0, 0).wait()
        gather(0)
        if n_steps > 1:
            idx_copy(1, 1).start()

    @pl.when(n + 1 < n_steps)
    def _():
        idx_copy(n + 1, nxt).wait()
        gather(nxt)

    @pl.when(n + 2 < n_steps)
    def _():
        idx_copy(n + 2, cur).start()

    def drain(t, carry):
        for k in range(2):
            _row_copy(y_hbm.at[0], buf_ref.at[cur, k, pl.ds(0, ROW_TILES)], row_sem.at[cur]).wait()
        return carry

    lax.fori_loop(0, TOK_TILE, drain, 0, unroll=8)

    wts = wts_ref[0]
    ffn = jnp.zeros((TOK_TILE, D_MODEL), F32)
    for k in range(2):
        yk = jnp.concatenate([buf_ref[cur, k, pl.ds(s, TOK_TILE, stride=ROW_TILES), :] for s in range(ROW_TILES)],
                             axis=1)
        ffn = ffn + wts[:, k:k + 1] * yk
    h = h_ref[0] + mod_ref[0, 5:6, :] * ffn
    o_ref[0] = h * lax.rsqrt(jnp.mean(h * h, axis=-1, keepdims=True) + EPS) * fg_ref[...]


def _combine(dest, y_rows, h, wts, mod, fg):
    B, S, _ = h.shape
    tps = S // TOK_TILE
    row = lambda b, i: (b, i, 0)
    any_spec = pl.BlockSpec(memory_space=pl.ANY)
    return pl.pallas_call(
        functools.partial(_combine_kernel, tiles_per_seq=tps, n_steps=B * tps),
        grid=(B, tps),
        in_specs=[any_spec, any_spec,
                  pl.BlockSpec((1, TOK_TILE, D_MODEL), row),
                  pl.BlockSpec((1, TOK_TILE, LANES), row),
                  pl.BlockSpec((1, N_MOD, D_MODEL), lambda b, i: (b, 0, 0)),
                  pl.BlockSpec((1, D_MODEL), lambda b, i: (0, 0))],
        out_specs=pl.BlockSpec((1, TOK_TILE, D_MODEL), row),
        out_shape=jax.ShapeDtypeStruct((B, S, D_MODEL), F32),
        scratch_shapes=[pltpu.SMEM((2, SMEM_IDX), jnp.int32),
                        pltpu.VMEM((2, 2, TOK_TILE * ROW_TILES, LANES), F32),
                        pltpu.SemaphoreType.DMA((2,)), pltpu.SemaphoreType.DMA((2,))],
        compiler_params=_params(("arbitrary", "arbitrary")),
        name="combine",
    )(dest, y_rows, h, wts, mod, fg)


def _rope_tables(S):
    half = HEAD_DIM // 4
    inv = ROPE_THETA ** (-jnp.arange(half, dtype=F32) / half)
    rows = S // GRID_W

    def axis_tables(n):
        ang = jnp.arange(n, dtype=F32)[:, None] * inv[None, :]
        c, s = jnp.cos(ang), jnp.sin(ang)
        return jnp.concatenate([c, c], axis=1), jnp.concatenate([-s, s], axis=1)

    def per_token(r, c):
        r = jnp.broadcast_to(r[:, None, :], (rows, GRID_W, 2 * half))
        c = jnp.broadcast_to(c[None, :, :], (rows, GRID_W, 2 * half))
        return jnp.tile(jnp.concatenate([r, c], axis=-1).reshape(S, HEAD_DIM), (1, LANES // HEAD_DIM))

    (rc, rs), (cc, cs) = axis_tables(rows), axis_tables(GRID_W)
    return per_token(rc, cc), per_token(rs, cs)


def _moe_ffn(n2, h1, eid, wts, mod, w1, w3, w2, final_g):
    B, S, _ = h1.shape
    T = B * S
    n_blocks = (2 * T + MOE_ROWS - 1) // MOE_ROWS + N_EXPERTS
    R = n_blocks * MOE_ROWS
    eid2 = eid.reshape(T, LANES)
    rank, counts = _rank(eid2, 512)
    counts = counts[0, :N_EXPERTS]
    padded = (counts + MOE_ROWS - 1) // MOE_ROWS * MOE_ROWS
    pad_ends = jnp.cumsum(padded)
    pad_starts = pad_ends - padded
    e2 = eid2[:, :2]
    onehot = e2[:, :, None] == jnp.arange(N_EXPERTS, dtype=jnp.int32)[None, None, :]
    dest = (jnp.sum(jnp.where(onehot, pad_starts[None, None, :], 0), axis=-1) + rank[:, :2]).reshape(2 * T)
    blk_start = jnp.arange(n_blocks, dtype=jnp.int32) * MOE_ROWS
    blk_eid = jnp.minimum(jnp.sum(pad_ends[None, :] <= blk_start[:, None], axis=1), N_EXPERTS - 1).astype(jnp.int32)
    n_used = (pad_ends[-1:] // MOE_ROWS).astype(jnp.int32)
    xin = _dispatch(pad_ends.astype(jnp.int32), n_used, dest, n2.reshape(T, ROW_TILES, LANES), n_blocks)
    y_rows = _moe(blk_eid, n_used, xin.reshape(R * ROW_TILES, LANES), w1, w3, w2)
    return _combine(dest, y_rows.reshape(R, ROW_TILES, LANES), h1, wts, mod, final_g)


def _trunk(x, mod, p):
    B, S, _ = x.shape
    cos, sin = _rope_tables(S)
    q, k, v, z, xbc, dtr = _in_proj(x, mod, p["g1"], p["w_in"], p["qg"], p["kg"], cos, sin, p["seg"], tm=512)
    attn = _attention(q, k, v, p["attn_g"], tq=min(256, S), tk=min(1024, S))
    xact = _conv(xbc, p["conv_w"], p["conv_b"], tr=512)
    y_b = _ssd(xact, dtr, p["dt_bias"], p["a_log"], None, rev=True, rows=512)
    ssm = _ssd(xact, dtr, p["dt_bias"], p["a_log"], (y_b, z, p["d_skip"], p["ssm_g"]), rev=False, rows=512)
    h1, n2, eid, wts = _out_proj(attn, ssm, x, mod, p["w_out_a"], p["w_out_s"], p["g2"], p["w_r_hi"], p["w_r_lo"],
                                  p["b_r"], tm=512)
    return _moe_ffn(n2, h1, eid, wts, mod, p["w1"], p["w3"], p["w2"], p["final_g"])


def kernel(x_prompt, x_sample, c_prompt, c_sample, w_ada, b_ada, norm1_g, w_in, q_norm_g, k_norm_g, attn_out_g,
           conv_w, conv_b, a_log, dt_bias, d_skip, ssm_norm_g, w_out, norm2_g, w_rg, b_rg, w_re, b_re, w1, w3, w2,
           final_g):
    l = 0
    Bp, Bs = c_prompt.shape[0], c_sample.shape[0]
    rows = (Bp + Bs + SUBLANES - 1) // SUBLANES * SUBLANES
    c_all = jnp.zeros((rows, D_MODEL), F32).at[:Bp].set(c_prompt).at[Bp:Bp + Bs].set(c_sample)
    mod = _modulation(c_all, w_ada[l], b_ada[l].reshape(1, -1))
    mod_p = mod[:Bp].reshape(Bp, N_MOD, D_MODEL)
    mod_s = mod[Bp:Bp + Bs].reshape(Bs, N_MOD, D_MODEL)

    def lane_pad(a):
        return jnp.pad(a.reshape(1, -1), ((0, 0), (0, LANES - a.size)))

    head_id = jnp.arange(LANES) // HEAD_DIM
    w_r = jnp.concatenate([w_rg[l]] + [w_re[l, g] for g in range(N_EGROUPS)], axis=1)
    b_r = jnp.concatenate([b_rg[l]] + [b_re[l, g] for g in range(N_EGROUPS)])
    w_r = jnp.pad(w_r, ((0, 0), (0, LANES - w_r.shape[1])))
    w_r_hi = w_r.astype(BF16)
    p = {
        "g1": norm1_g[l].reshape(1, -1),
        "w_in": jnp.pad(w_in[l], ((0, 0), (0, D_IN_PAD - D_IN_PROJ))).astype(BF16),
        "qg": jnp.tile(q_norm_g[l] * (HEAD_DIM ** -0.5), N_HEADS).reshape(1, -1),
        "kg": jnp.tile(k_norm_g[l], N_KV_HEADS).reshape(1, -1),
        "seg": (head_id[:, None] == head_id[None, :]).astype(BF16),
        "attn_g": attn_out_g[l].reshape(1, -1),
        "conv_w": conv_w[l].reshape(CONV_K, D_XBC),
        "conv_b": conv_b[l].reshape(1, -1),
        "dt_bias": lane_pad(dt_bias[l]),
        "a_log": lane_pad(a_log[l]),
        "d_skip": jnp.repeat(d_skip[l], SSM_HEAD_DIM).reshape(1, -1),
        "ssm_g": ssm_norm_g[l].reshape(1, -1),
        "w_out_a": w_out[l, :D_ATTN].astype(BF16),
        "w_out_s": w_out[l, D_ATTN:].astype(BF16),
        "g2": norm2_g[l].reshape(1, -1),
        "w_r_hi": w_r_hi,
        "w_r_lo": (w_r - w_r_hi.astype(F32)).astype(BF16),
        "b_r": lane_pad(b_r),
        "w1": w1[l],
        "w3": w3[l],
        "w2": w2[l],
        "final_g": final_g.reshape(1, -1),
    }
    return _trunk(x_prompt, mod_p, p), _trunk(x_sample, mod_s, p)
```

```python
import functools

import jax
import jax.numpy as jnp
from jax import lax
from jax.experimental import pallas as pl
from jax.experimental.pallas import tpu as pltpu

F32 = jnp.float32
BF16 = jnp.bfloat16
HIGHEST = lax.Precision.HIGHEST

D_MODEL = 1024
GRID_W = 64
EPS = 1e-6
N_HEADS = 8
N_KV_HEADS = 2
HEAD_DIM = 64
D_ATTN = N_HEADS * HEAD_DIM
D_KV = N_KV_HEADS * HEAD_DIM
ROPE_THETA = 10000.0
SSM_HEADS = 8
SSM_HEAD_DIM = 64
D_SSM = SSM_HEADS * SSM_HEAD_DIM
SSM_STATE = 64
SSM_GROUPS = 2
CONV_K = 5
CHUNK = 128
D_XBC = D_SSM + 2 * SSM_GROUPS * SSM_STATE
D_IN_PROJ = D_ATTN + 2 * D_KV + 2 * D_SSM + 2 * SSM_GROUPS * SSM_STATE + 2 * SSM_HEADS
N_EGROUPS = 4
EXPERTS_PER_GROUP = 8
N_EXPERTS = N_EGROUPS * EXPERTS_PER_GROUP
D_EXPERT = 512
N_MOD = 6

LANES = 128
SUBLANES = 8
ROW_TILES = D_MODEL // LANES
D_IN_PAD = (D_IN_PROJ + LANES - 1) // LANES * LANES
SMEM_IDX = 1024
TOK_TILE = SMEM_IDX // 2
MOE_ROWS = 512
VMEM_LIMIT = 48 * 1024 * 1024

_Q0, _K0, _V0, _Z0, _X0, _DT0 = 0, D_ATTN, D_ATTN + D_KV, D_ATTN + 2 * D_KV, D_ATTN + 2 * D_KV + D_SSM, \
    D_ATTN + 2 * D_KV + D_SSM + D_XBC


def _params(sem):
    return pltpu.CompilerParams(dimension_semantics=sem, vmem_limit_bytes=VMEM_LIMIT)


def _silu(x):
    return x * jax.nn.sigmoid(x)


def _mod_kernel(c_ref, w_ref, b_ref, o_ref):
    o_ref[...] = jnp.dot(_silu(c_ref[...]), w_ref[...], precision=HIGHEST,
                         preferred_element_type=F32) + b_ref[...]


def _modulation(c, w, b):
    rows, n = c.shape[0], w.shape[1]
    tn = 1024
    return pl.pallas_call(
        _mod_kernel,
        grid=(n // tn,),
        in_specs=[pl.BlockSpec((rows, D_MODEL), lambda j: (0, 0)),
                  pl.BlockSpec((D_MODEL, tn), lambda j: (0, j)),
                  pl.BlockSpec((1, tn), lambda j: (0, j))],
        out_specs=pl.BlockSpec((rows, tn), lambda j: (0, j)),
        out_shape=jax.ShapeDtypeStruct((rows, n), F32),
        compiler_params=_params(("arbitrary",)),
        name="mod",
    )(c, w, b)


def _inproj_kernel(x_ref, mod_ref, g1_ref, w_ref, qg_ref, kg_ref, cos_ref, sin_ref, seg_ref,
                   q_ref, k_ref, v_ref, z_ref, xbc_ref, dt_ref):
    x = x_ref[0]
    ms = jnp.mean(x * x, axis=-1, keepdims=True)
    xn = x * lax.rsqrt(ms + EPS) * g1_ref[...]
    n1 = xn * (1.0 + mod_ref[0, 1:2, :]) + mod_ref[0, 0:1, :]
    proj = jnp.dot(n1.astype(BF16), w_ref[...], preferred_element_type=F32)

    cos = cos_ref[...]
    sin = sin_ref[...]
    seg = seg_ref[...]
    lane = lax.broadcasted_iota(jnp.int32, cos.shape, 1)
    low_half = (lane % 32) < 16

    def norm_rope(xc, gain):
        sq = xc * xc
        hi = sq.astype(BF16)
        lo = (sq - hi.astype(F32)).astype(BF16)
        ss = (jnp.dot(hi, seg, preferred_element_type=F32)
              + jnp.dot(lo, seg, preferred_element_type=F32))
        xg = xc * lax.rsqrt(ss * (1.0 / HEAD_DIM) + EPS) * gain
        rot = jnp.where(low_half, pltpu.roll(xg, LANES - 16, 1), pltpu.roll(xg, 16, 1))
        return xg * cos + rot * sin

    for c in range(D_ATTN // LANES):
        sl = slice(c * LANES, (c + 1) * LANES)
        q_ref[0, :, sl] = norm_rope(proj[:, _Q0 + c * LANES:_Q0 + (c + 1) * LANES], qg_ref[:, sl]).astype(BF16)
    k_ref[0] = norm_rope(proj[:, _K0:_V0], kg_ref[...]).astype(BF16)
    v = proj[:, _V0:_Z0]
    for j in range(N_KV_HEADS):
        mine = (lane >= HEAD_DIM * j) & (lane < HEAD_DIM * (j + 1))
        v_ref[0, j] = jnp.where(mine, v, 1.0).astype(BF16)
    z_ref[0] = proj[:, _Z0:_X0]
    xbc_ref[0] = proj[:, _X0:_DT0]
    dt_ref[0] = proj[:, _DT0:D_IN_PAD]


def _in_proj(x, mod, g1, w_in, qg, kg, cos, sin, seg, tm):
    B, S, _ = x.shape
    row = lambda b, i: (b, i, 0)
    const = lambda b, i: (0, 0)
    outs = [(D_ATTN, BF16), (D_KV, BF16), None, (D_SSM, F32), (D_XBC, F32), (LANES, F32)]
    out_specs = [pl.BlockSpec((1, tm, o[0]), row) if o else
                 pl.BlockSpec((1, N_KV_HEADS, tm, D_KV), lambda b, i: (b, 0, i, 0)) for o in outs]
    out_shape = [jax.ShapeDtypeStruct((B, S, o[0]), o[1]) if o else
                 jax.ShapeDtypeStruct((B, N_KV_HEADS, S, D_KV), BF16) for o in outs]
    return pl.pallas_call(
        _inproj_kernel,
        grid=(B, S // tm),
        in_specs=[pl.BlockSpec((1, tm, D_MODEL), row),
                  pl.BlockSpec((1, N_MOD, D_MODEL), lambda b, i: (b, 0, 0)),
                  pl.BlockSpec((1, D_MODEL), const),
                  pl.BlockSpec((D_MODEL, D_IN_PAD), const),
                  pl.BlockSpec((1, D_ATTN), const),
                  pl.BlockSpec((1, D_KV), const),
                  pl.BlockSpec((tm, LANES), lambda b, i: (i, 0)),
                  pl.BlockSpec((tm, LANES), lambda b, i: (i, 0)),
                  pl.BlockSpec((LANES, LANES), const)],
        out_specs=out_specs,
        out_shape=out_shape,
        compiler_params=_params(("parallel", "parallel")),
        name="in_proj",
    )(x, mod, g1, w_in, qg, kg, cos, sin, seg)


def _attn_kernel(q_ref, k_ref, v_ref, g_ref, o_ref, qs_ref, m_ref, acc_ref, *, tq, tk, nk):
    G = N_HEADS // N_KV_HEADS
    lane = lax.broadcasted_iota(jnp.int32, (tq, LANES), 1)
    qf = q_ref[0].astype(F32)
    for j in range(N_KV_HEADS):
        pieces = []
        for g in range(G):
            h = G * j + g
            chunk = qf[:, (h // 2) * LANES:(h // 2 + 1) * LANES]
            if h % 2 != j:
                chunk = pltpu.roll(chunk, HEAD_DIM, 1)
            keep = (lane >= HEAD_DIM * j) & (lane < HEAD_DIM * (j + 1))
            pieces.append(jnp.where(keep, chunk, 0.0))
        qs_ref[j] = jnp.concatenate(pieces, axis=0).astype(BF16)
    m_ref[...] = jnp.full(m_ref.shape, -jnp.inf, F32)
    acc_ref[...] = jnp.zeros(acc_ref.shape, F32)

    def body(c, carry):
        off = pl.multiple_of(c * tk, tk)
        kc = k_ref[0, pl.ds(off, tk), :]
        for j in range(N_KV_HEADS):
            s = lax.dot_general(qs_ref[j], kc, (((1,), (1,)), ((), ())), preferred_element_type=F32)
            m_prev = m_ref[j]
            m_new = jnp.maximum(m_prev, jnp.max(s, axis=1, keepdims=True))
            alpha = jnp.exp(m_prev - m_new)
            p = jnp.exp(s - jnp.concatenate([m_new] * (tk // LANES), axis=1))
            acc_ref[j] = alpha * acc_ref[j] + jnp.dot(p.astype(BF16), v_ref[0, j, pl.ds(off, tk), :],
                                                      preferred_element_type=F32)
            m_ref[j] = m_new
        return carry

    lax.fori_loop(0, nk, body, 0, unroll=4)

    o = [acc_ref[j] / pltpu.roll(acc_ref[j], HEAD_DIM, 1) for j in range(N_KV_HEADS)]

    def head(h, half):
        j, g = divmod(h, G)
        piece = o[j][g * tq:(g + 1) * tq, :]
        return pltpu.roll(piece, HEAD_DIM, 1) if j != half else piece

    of = jnp.concatenate(
        [jnp.where(lane < HEAD_DIM, head(2 * c, 0), head(2 * c + 1, 1)) for c in range(N_HEADS // 2)], axis=1)
    ms = jnp.mean(of * of, axis=-1, keepdims=True)
    o_ref[0] = (of * lax.rsqrt(ms + EPS) * g_ref[...]).astype(BF16)


def _attention(q, k, v, g, tq, tk):
    B, S, _ = q.shape
    G = N_HEADS // N_KV_HEADS
    kern = functools.partial(_attn_kernel, tq=tq, tk=tk, nk=S // tk)
    return pl.pallas_call(
        kern,
        grid=(B, S // tq),
        in_specs=[pl.BlockSpec((1, tq, D_ATTN), lambda b, i: (b, i, 0)),
                  pl.BlockSpec((1, S, D_KV), lambda b, i: (b, 0, 0)),
                  pl.BlockSpec((1, N_KV_HEADS, S, D_KV), lambda b, i: (b, 0, 0, 0)),
                  pl.BlockSpec((1, D_ATTN), lambda b, i: (0, 0))],
        out_specs=pl.BlockSpec((1, tq, D_ATTN), lambda b, i: (b, i, 0)),
        out_shape=jax.ShapeDtypeStruct((B, S, D_ATTN), BF16),
        scratch_shapes=[pltpu.VMEM((N_KV_HEADS, G * tq, LANES), BF16),
                        pltpu.VMEM((N_KV_HEADS, G * tq, LANES), F32),
                        pltpu.VMEM((N_KV_HEADS, G * tq, LANES), F32)],
        compiler_params=_params(("parallel", "parallel")),
        name="attn",
    )(q, k, v, g)


def _conv_kernel(xp_ref, xc_ref, xn_ref, w_ref, b_ref, o_ref, ext_ref, *, tr, nblk):
    i = pl.program_id(1)
    pad = CONV_K // 2
    ext_ref[0:SUBLANES, :] = jnp.where(i > 0, xp_ref[0], 0.0)
    ext_ref[SUBLANES:SUBLANES + tr, :] = xc_ref[0]
    ext_ref[SUBLANES + tr:, :] = jnp.where(i < nblk - 1, xn_ref[0], 0.0)
    acc = jnp.zeros((tr, D_XBC), F32) + b_ref[...]
    for kk in range(CONV_K):
        acc = acc + w_ref[kk:kk + 1, :] * ext_ref[SUBLANES - pad + kk:SUBLANES - pad + kk + tr, :]
    o_ref[0] = _silu(acc)


def _conv(xbc, w, b, tr):
    B, S, _ = xbc.shape
    nblk = S // tr
    per = tr // SUBLANES
    kern = functools.partial(_conv_kernel, tr=tr, nblk=nblk)
    return pl.pallas_call(
        kern,
        grid=(B, nblk),
        in_specs=[pl.BlockSpec((1, SUBLANES, D_XBC), lambda b, i: (b, jnp.maximum(i * per - 1, 0), 0)),
                  pl.BlockSpec((1, tr, D_XBC), lambda b, i: (b, i, 0)),
                  pl.BlockSpec((1, SUBLANES, D_XBC),
                               lambda b, i: (b, jnp.minimum((i + 1) * per, S // SUBLANES - 1), 0)),
                  pl.BlockSpec((CONV_K, D_XBC), lambda b, i: (0, 0)),
                  pl.BlockSpec((1, D_XBC), lambda b, i: (0, 0))],
        out_specs=pl.BlockSpec((1, tr, D_XBC), lambda b, i: (b, i, 0)),
        out_shape=jax.ShapeDtypeStruct((B, S, D_XBC), F32),
        scratch_shapes=[pltpu.VMEM((tr + 2 * SUBLANES, D_XBC), F32)],
        compiler_params=_params(("parallel", "parallel")),
        name="conv",
    )(xbc, xbc, xbc, w, b)


def _ssd_kernel(*refs, rev, rows):
    if rev:
        xbc_ref, dt_ref, dtb_ref, alog_ref, o_ref, st_ref = refs
    else:
        xbc_ref, dt_ref, dtb_ref, alog_ref, yb_ref, z_ref, dsk_ref, ng_ref, o_ref, st_ref = refs
    L = CHUNK
    d = 1 if rev else 0
    edge = 0 if rev else L - 1
    nchunk = rows // L

    @pl.when(pl.program_id(1) == 0)
    def _():
        st_ref[...] = jnp.zeros(st_ref.shape, F32)

    r_i = lax.broadcasted_iota(jnp.int32, (L, L), 0)
    c_i = lax.broadcasted_iota(jnp.int32, (L, L), 1)
    causal = (c_i >= r_i) if rev else (r_i >= c_i)
    tri = causal.astype(F32)
    low = c_i < SSM_HEAD_DIM
    a_row = -jnp.exp(alog_ref[...])

    def chunk(ci, carry):
        cc = (nchunk - 1 - ci) if rev else ci
        off = pl.multiple_of(cc * L, L)
        xbc = xbc_ref[0, pl.ds(off, L), :]
        xs = xbc[:, :D_SSM]
        bm = xbc[:, D_SSM:D_SSM + LANES]
        cm = xbc[:, D_SSM + LANES:]
        x_dt = dt_ref[0, pl.ds(off, L), :] + dtb_ref[...]
        dt_all = jnp.maximum(x_dt, 0.0) + jnp.log(1.0 + jnp.exp(-jnp.abs(x_dt)))
        a_all = dt_all * a_row
        acs = jnp.dot(tri, a_all, precision=HIGHEST, preferred_element_type=F32)
        acs_t = acs.T
        bm_b = bm.astype(BF16)
        bm_t = bm.T.astype(BF16)
        cg = [jnp.where(low, cm, 0.0).astype(BF16), jnp.where(low, 0.0, cm).astype(BF16)]
        cb = [lax.dot_general(cg[g], bm_b, (((1,), (1,)), ((), ())), preferred_element_type=F32)
              for g in range(SSM_GROUPS)]
        ys = []
        for c in range(SSM_HEADS // 2):
            g = (2 * c) // (SSM_HEADS // SSM_GROUPS)
            ja, jb = d * SSM_HEADS + 2 * c, d * SSM_HEADS + 2 * c + 1
            sl = slice(c * LANES, (c + 1) * LANES)
            dt_pair = jnp.where(low, dt_all[:, ja:ja + 1], dt_all[:, jb:jb + 1])
            acs_pair = jnp.where(low, acs[:, ja:ja + 1], acs[:, jb:jb + 1])
            xdt = xs[:, sl] * dt_pair
            xdt_b = xdt.astype(BF16)
            y_heads = []
            for jh in (ja, jb):
                seg = acs[:, jh:jh + 1] - acs_t[jh:jh + 1, :]
                lmat = jnp.exp(jnp.where(causal, seg, -jnp.inf))
                y_heads.append(jnp.dot((cb[g] * lmat).astype(BF16), xdt_b, preferred_element_type=F32))
            y = jnp.where(low, y_heads[0], y_heads[1])
            st = st_ref[:, sl]
            y = y + jnp.dot(cg[g], st.astype(BF16), preferred_element_type=F32) * jnp.exp(acs_pair)
            ys.append(y)
            edge_row = acs_pair[edge:edge + 1, :]
            xdec = (xdt * jnp.exp(edge_row - acs_pair)).astype(BF16)
            st_ref[:, sl] = jnp.exp(edge_row) * st + jnp.dot(bm_t, xdec, preferred_element_type=F32)
        y = jnp.concatenate(ys, axis=1)
        if rev:
            o_ref[0, pl.ds(off, L), :] = y
        else:
            y = y + yb_ref[0, pl.ds(off, L), :] + dsk_ref[...] * xs
            y = y * _silu(z_ref[0, pl.ds(off, L), :])
            gw = D_SSM // SSM_GROUPS
            outs = []
            for g in range(SSM_GROUPS):
                yg = y[:, g * gw:(g + 1) * gw]
                outs.append(yg * lax.rsqrt(jnp.mean(yg * yg, axis=-1, keepdims=True) + EPS))
            o_ref[0, pl.ds(off, L), :] = (jnp.concatenate(outs, axis=1) * ng_ref[...]).astype(BF16)
        return carry

    lax.fori_loop(0, nchunk, chunk, 0, unroll=2)


def _ssd(xbc, dt, dtb, alog, extra, rev, rows):
    B, S, _ = xbc.shape
    nblk = S // rows
    blk = (lambda b, i: (b, nblk - 1 - i, 0)) if rev else (lambda b, i: (b, i, 0))
    const = lambda b, i: (0, 0)
    in_specs = [pl.BlockSpec((1, rows, D_XBC), blk),
                pl.BlockSpec((1, rows, LANES), blk),
                pl.BlockSpec((1, LANES), const),
                pl.BlockSpec((1, LANES), const)]
    args = [xbc, dt, dtb, alog]
    if not rev:
        yb, z, dsk, ng = extra
        in_specs += [pl.BlockSpec((1, rows, D_SSM), blk), pl.BlockSpec((1, rows, D_SSM), blk),
                     pl.BlockSpec((1, D_SSM), const), pl.BlockSpec((1, D_SSM), const)]
        args += [yb, z, dsk, ng]
    return pl.pallas_call(
        functools.partial(_ssd_kernel, rev=rev, rows=rows),
        grid=(B, nblk),
        in_specs=in_specs,
        out_specs=pl.BlockSpec((1, rows, D_SSM), blk),
        out_shape=jax.ShapeDtypeStruct((B, S, D_SSM), F32 if rev else BF16),
        scratch_shapes=[pltpu.VMEM((SSM_GROUPS * SSM_STATE, D_SSM), F32)],
        compiler_params=_params(("parallel", "arbitrary")),
        name="ssd_bwd" if rev else "ssd_fwd",
    )(*args)


def _outproj_kernel(a_ref, s_ref, x_ref, mod_ref, wa_ref, ws_ref, g2_ref, wrh_ref, wrl_ref, br_ref,
                    h_ref, n2_ref, eid_ref, wts_ref):
    mix = (jnp.dot(a_ref[0], wa_ref[...], preferred_element_type=F32)
           + jnp.dot(s_ref[0], ws_ref[...], preferred_element_type=F32))
    h = x_ref[0] + mod_ref[0, 2:3, :] * mix
    h_ref[0] = h
    hn = h * lax.rsqrt(jnp.mean(h * h, axis=-1, keepdims=True) + EPS) * g2_ref[...]
    n2 = hn * (1.0 + mod_ref[0, 4:5, :]) + mod_ref[0, 3:4, :]
    for s in range(ROW_TILES):
        n2_ref[0, pl.ds(s, n2.shape[0], stride=ROW_TILES), :] = n2[:, s * LANES:(s + 1) * LANES]

    n_hi = n2.astype(BF16)
    n_lo = (n2 - n_hi.astype(F32)).astype(BF16)
    logits = (jnp.dot(n_hi, wrh_ref[...], preferred_element_type=F32)
              + (jnp.dot(n_lo, wrh_ref[...], preferred_element_type=F32)
                 + jnp.dot(n_hi, wrl_ref[...], preferred_element_type=F32))) + br_ref[...]
    lane = lax.broadcasted_iota(jnp.int32, logits.shape, 1)
    lane_f = lane.astype(F32)
    big = float(LANES)
    is_g = lane < N_EGROUPS
    gl = jnp.where(is_g, logits, -jnp.inf)
    gmax = jnp.max(gl, axis=1, keepdims=True)
    gidx = jnp.min(jnp.where(gl == gmax, lane_f, big), axis=1, keepdims=True)
    g_w = 1.0 / jnp.sum(jnp.where(is_g, jnp.exp(gl - gmax), 0.0), axis=1, keepdims=True)
    e_lo = N_EGROUPS + EXPERTS_PER_GROUP * gidx
    sel = (lane_f >= e_lo) & (lane_f < e_lo + EXPERTS_PER_GROUP)
    el = jnp.where(sel, logits, -jnp.inf)
    v1 = jnp.max(el, axis=1, keepdims=True)
    i1 = jnp.min(jnp.where(el == v1, lane_f, big), axis=1, keepdims=True)
    el2 = jnp.where(lane_f == i1, -jnp.inf, el)
    v2 = jnp.max(el2, axis=1, keepdims=True)
    i2 = jnp.min(jnp.where(el2 == v2, lane_f, big), axis=1, keepdims=True)
    t = jnp.exp(v2 - v1)
    den = 1.0 + t
    w_a = g_w * (1.0 / den)
    w_b = g_w * (t / den)
    eid_ref[0] = jnp.where(lane == 0, i1 - N_EGROUPS, jnp.where(lane == 1, i2 - N_EGROUPS, 0.0)).astype(jnp.int32)
    wts_ref[0] = jnp.where(lane == 0, w_a, jnp.where(lane == 1, w_b, 0.0))


def _out_proj(attn, ssm, x, mod, wa, ws, g2, wr_hi, wr_lo, br, tm):
    B, S, _ = x.shape
    row = lambda b, i: (b, i, 0)
    const = lambda b, i: (0, 0)
    return pl.pallas_call(
        _outproj_kernel,
        grid=(B, S // tm),
        in_specs=[pl.BlockSpec((1, tm, D_ATTN), row),
                  pl.BlockSpec((1, tm, D_SSM), row),
                  pl.BlockSpec((1, tm, D_MODEL), row),
                  pl.BlockSpec((1, N_MOD, D_MODEL), lambda b, i: (b, 0, 0)),
                  pl.BlockSpec((D_ATTN, D_MODEL), const),
                  pl.BlockSpec((D_SSM, D_MODEL), const),
                  pl.BlockSpec((1, D_MODEL), const),
                  pl.BlockSpec((D_MODEL, LANES), const),
                  pl.BlockSpec((D_MODEL, LANES), const),
                  pl.BlockSpec((1, LANES), const)],
        out_specs=[pl.BlockSpec((1, tm, D_MODEL), row),
                   pl.BlockSpec((1, tm * ROW_TILES, LANES), row),
                   pl.BlockSpec((1, tm, LANES), row), pl.BlockSpec((1, tm, LANES), row)],
        out_shape=[jax.ShapeDtypeStruct((B, S, D_MODEL), F32),
                   jax.ShapeDtypeStruct((B, S * ROW_TILES, LANES), F32),
                   jax.ShapeDtypeStruct((B, S, LANES), jnp.int32), jax.ShapeDtypeStruct((B, S, LANES), F32)],
        compiler_params=_params(("parallel", "parallel")),
        name="out_proj",
    )(attn, ssm, x, mod, wa, ws, g2, wr_hi, wr_lo, br)


def _rank_kernel(eid_ref, rank_ref, cnt_ref, carry_ref, *, tt):
    @pl.when(pl.program_id(0) == 0)
    def _():
        carry_ref[...] = jnp.zeros(carry_ref.shape, F32)

    eid = eid_ref[...]
    lane = lax.broadcasted_iota(jnp.int32, eid.shape, 1)
    r_i = lax.broadcasted_iota(jnp.int32, (tt, tt), 0)
    c_i = lax.broadcasted_iota(jnp.int32, (tt, tt), 1)
    before = (c_i < r_i).astype(BF16)
    carry = carry_ref[0:1, :]
    ranks = []
    for k in range(2):
        oh = lane == eid[:, k:k + 1]
        oh_f = oh.astype(F32)
        pref = jnp.dot(before, oh.astype(BF16), preferred_element_type=F32) + carry
        ranks.append(jnp.sum(oh_f * pref, axis=1, keepdims=True))
        carry = carry + jnp.sum(oh_f, axis=0, keepdims=True)
    carry_ref[0:1, :] = carry
    rank_ref[...] = jnp.where(lane == 0, ranks[0], jnp.where(lane == 1, ranks[1], 0.0)).astype(jnp.int32)
    cnt_ref[...] = jnp.broadcast_to(carry, cnt_ref.shape).astype(jnp.int32)


def _rank(eid, tt):
    T = eid.shape[0]
    return pl.pallas_call(
        functools.partial(_rank_kernel, tt=tt),
        grid=(T // tt,),
        in_specs=[pl.BlockSpec((tt, LANES), lambda i: (i, 0))],
        out_specs=[pl.BlockSpec((tt, LANES), lambda i: (i, 0)),
                   pl.BlockSpec((SUBLANES, LANES), lambda i: (0, 0))],
        out_shape=[jax.ShapeDtypeStruct((T, LANES), jnp.int32),
                   jax.ShapeDtypeStruct((SUBLANES, LANES), jnp.int32)],
        scratch_shapes=[pltpu.VMEM((SUBLANES, LANES), F32)],
        compiler_params=_params(("arbitrary",)),
        name="rank",
    )(eid)


def _row_copy(src, dst, sem):
    return pltpu.make_async_copy(src, dst, sem)


def _dispatch_kernel(pend_ref, nused_ref, dest_hbm, n2_ref, xin_hbm, idx_ref, zero_ref, idx_sem, row_sem, zero_sem,
                     *, n_blocks):
    i = pl.program_id(0)

    @pl.when(i == 0)
    def _():
        zero_ref[...] = jnp.zeros(zero_ref.shape, F32)

        def zero_copy(blk):
            rows = pl.ds(pl.multiple_of(blk * MOE_ROWS, MOE_ROWS), MOE_ROWS)
            return pltpu.make_async_copy(zero_ref, xin_hbm.at[rows], zero_sem)

        def region_tail(fn):
            def per_expert(e, carry):
                start = jnp.where(e > 0, pend_ref[jnp.maximum(e - 1, 0)], 0)

                @pl.when(pend_ref[e] > start)
                def _():
                    fn(zero_copy(pend_ref[e] // MOE_ROWS - 1))

                return carry

            lax.fori_loop(0, N_EXPERTS, per_expert, 0)

            def per_unused(blk, carry):
                fn(zero_copy(blk))
                return carry

            lax.fori_loop(nused_ref[0], n_blocks, per_unused, 0)

        region_tail(lambda cp: cp.start())
        region_tail(lambda cp: cp.wait())

    idx_cp = pltpu.make_async_copy(dest_hbm.at[pl.ds(pl.multiple_of(i * SMEM_IDX, SMEM_IDX), SMEM_IDX)],
                                   idx_ref, idx_sem)
    idx_cp.start()
    idx_cp.wait()

    def issue(t, carry):
        for k in range(2):
            _row_copy(n2_ref.at[t], xin_hbm.at[idx_ref[2 * t + k]], row_sem).start(priority=k)
        return carry

    lax.fori_loop(0, TOK_TILE, issue, 0, unroll=8)

    def drain(t, carry):
        for k in range(2):
            _row_copy(n2_ref.at[0], xin_hbm.at[0], row_sem).wait()
        return carry

    lax.fori_loop(0, TOK_TILE, drain, 0, unroll=8)


def _dispatch(pad_ends, n_used, dest, n2_rows, n_blocks):
    T = n2_rows.shape[0]
    any_spec = pl.BlockSpec(memory_space=pl.ANY)
    grid_spec = pltpu.PrefetchScalarGridSpec(
        num_scalar_prefetch=2,
        grid=(T // TOK_TILE,),
        in_specs=[any_spec, pl.BlockSpec((TOK_TILE, ROW_TILES, LANES), lambda i, pe, nu: (i, 0, 0))],
        out_specs=any_spec,
        scratch_shapes=[pltpu.SMEM((SMEM_IDX,), jnp.int32), pltpu.VMEM((MOE_ROWS, ROW_TILES, LANES), F32),
                        pltpu.SemaphoreType.DMA, pltpu.SemaphoreType.DMA, pltpu.SemaphoreType.DMA],
    )
    return pl.pallas_call(
        functools.partial(_dispatch_kernel, n_blocks=n_blocks),
        grid_spec=grid_spec,
        out_shape=jax.ShapeDtypeStruct((n_blocks * MOE_ROWS, ROW_TILES, LANES), F32),
        compiler_params=_params(("arbitrary",)),
        name="dispatch",
    )(pad_ends, n_used, dest, n2_rows)


def _moe_kernel(beid_ref, nused_ref, xin_ref, w1_ref, w3_ref, w2_ref, y_ref, w1b_ref, w3b_ref, w2b_ref):
    i = pl.program_id(0)

    @pl.when((i == 0) | (beid_ref[i] != beid_ref[jnp.maximum(i - 1, 0)]))
    def _():
        w1b_ref[...] = w1_ref[0].astype(BF16)
        w3b_ref[...] = w3_ref[0].astype(BF16)
        w2b_ref[...] = w2_ref[0].astype(BF16)

    @pl.when(i < nused_ref[0])
    def _():
        x = jnp.concatenate([xin_ref[pl.ds(s, MOE_ROWS, stride=ROW_TILES), :] for s in range(ROW_TILES)],
                            axis=1).astype(BF16)
        h1 = jnp.dot(x, w1b_ref[...], preferred_element_type=F32)
        h3 = jnp.dot(x, w3b_ref[...], preferred_element_type=F32)
        y = jnp.dot((_silu(h1) * h3).astype(BF16), w2b_ref[...], preferred_element_type=F32)
        for s in range(ROW_TILES):
            y_ref[pl.ds(s, MOE_ROWS, stride=ROW_TILES), :] = y[:, s * LANES:(s + 1) * LANES]

    @pl.when(i >= nused_ref[0])
    def _():
        y_ref[...] = jnp.zeros(y_ref.shape, F32)


def _moe(blk_eid, n_used, xin, w1, w3, w2):
    R = xin.shape[0] // ROW_TILES
    rows = MOE_ROWS * ROW_TILES
    grid_spec = pltpu.PrefetchScalarGridSpec(
        num_scalar_prefetch=2,
        grid=(R // MOE_ROWS,),
        in_specs=[pl.BlockSpec((rows, LANES), lambda i, be, nu: (jnp.minimum(i, nu[0] - 1), 0)),
                  pl.BlockSpec((1, D_MODEL, D_EXPERT), lambda i, be, nu: (be[i], 0, 0)),
                  pl.BlockSpec((1, D_MODEL, D_EXPERT), lambda i, be, nu: (be[i], 0, 0)),
                  pl.BlockSpec((1, D_EXPERT, D_MODEL), lambda i, be, nu: (be[i], 0, 0))],
        out_specs=pl.BlockSpec((rows, LANES), lambda i, be, nu: (i, 0)),
        scratch_shapes=[pltpu.VMEM((D_MODEL, D_EXPERT), BF16), pltpu.VMEM((D_MODEL, D_EXPERT), BF16),
                        pltpu.VMEM((D_EXPERT, D_MODEL), BF16)],
    )
    return pl.pallas_call(
        _moe_kernel,
        grid_spec=grid_spec,
        out_shape=jax.ShapeDtypeStruct(xin.shape, F32),
        compiler_params=_params(("arbitrary",)),
        name="moe",
    )(blk_eid, n_used, xin, w1, w3, w2)


def _combine_kernel(dest_hbm, y_hbm, h_ref, wts_ref, mod_ref, fg_ref, o_ref, idx_ref, buf_ref, idx_sem, row_sem,
                    *, tiles_per_seq, n_steps):
    n = pl.program_id(0) * tiles_per_seq + pl.program_id(1)

    def idx_copy(step, slot):
        return pltpu.make_async_copy(
            dest_hbm.at[pl.ds(pl.multiple_of(step * SMEM_IDX, SMEM_IDX), SMEM_IDX)], idx_ref.at[slot],
            idx_sem.at[slot])

    def gather(slot):
        def issue(t, carry):
            for k in range(2):
                dst = buf_ref.at[slot, k, pl.ds(pl.multiple_of(t * ROW_TILES, ROW_TILES), ROW_TILES)]
                _row_copy(y_hbm.at[idx_ref[slot, 2 * t + k]], dst, row_sem.at[slot]).start(priority=k)
            return carry

        lax.fori_loop(0, TOK_TILE, issue, 0, unroll=8)

    def step(cur):
        nxt = 1 - cur
        if cur == 0:
            @pl.when(n == 0)
            def _():
                idx_copy(0, 0).start()
                idx_copy(0, 0).wait()
                gather(0)
                if n_steps > 1:
                    idx_copy(1, 1).start()

        @pl.when(n + 1 < n_steps)
        def _():
            idx_copy(n + 1, nxt).wait()
            gather(nxt)

        @pl.when(n + 2 < n_steps)
        def _():
            idx_copy(n + 2, cur).start()

        def drain(t, carry):
            for k in range(2):
                _row_copy(y_hbm.at[0], buf_ref.at[cur, k, pl.ds(0, ROW_TILES)], row_sem.at[cur]).wait()
            return carry

        lax.fori_loop(0, TOK_TILE, drain, 0, unroll=8)

        wts = wts_ref[0]
        ffn = jnp.zeros((TOK_TILE, D_MODEL), F32)
        for k in range(2):
            yk = jnp.concatenate(
                [buf_ref[cur, k, pl.ds(s, TOK_TILE, stride=ROW_TILES), :] for s in range(ROW_TILES)], axis=1)
            ffn = ffn + wts[:, k:k + 1] * yk
        h = h_ref[0] + mod_ref[0, 5:6, :] * ffn
        o_ref[0] = h * lax.rsqrt(jnp.mean(h * h, axis=-1, keepdims=True) + EPS) * fg_ref[...]

    for parity in range(2):
        pl.when(n % 2 == parity)(functools.partial(step, parity))


def _combine(dest, y_rows, h, wts, mod, fg):
    B, S, _ = h.shape
    tps = S // TOK_TILE
    row = lambda b, i: (b, i, 0)
    any_spec = pl.BlockSpec(memory_space=pl.ANY)
    return pl.pallas_call(
        functools.partial(_combine_kernel, tiles_per_seq=tps, n_steps=B * tps),
        grid=(B, tps),
        in_specs=[any_spec, any_spec,
                  pl.BlockSpec((1, TOK_TILE, D_MODEL), row),
                  pl.BlockSpec((1, TOK_TILE, LANES), row),
                  pl.BlockSpec((1, N_MOD, D_MODEL), lambda b, i: (b, 0, 0)),
                  pl.BlockSpec((1, D_MODEL), lambda b, i: (0, 0))],
        out_specs=pl.BlockSpec((1, TOK_TILE, D_MODEL), row),
        out_shape=jax.ShapeDtypeStruct((B, S, D_MODEL), F32),
        scratch_shapes=[pltpu.SMEM((2, SMEM_IDX), jnp.int32),
                        pltpu.VMEM((2, 2, TOK_TILE * ROW_TILES, LANES), F32),
                        pltpu.SemaphoreType.DMA((2,)), pltpu.SemaphoreType.DMA((2,))],
        compiler_params=_params(("arbitrary", "arbitrary")),
        name="combine",
    )(dest, y_rows, h, wts, mod, fg)


def _rope_tables(S):
    half = HEAD_DIM // 4
    inv = ROPE_THETA ** (-jnp.arange(half, dtype=F32) / half)
    rows = S // GRID_W

    def axis_tables(n):
        ang = jnp.arange(n, dtype=F32)[:, None] * inv[None, :]
        c, s = jnp.cos(ang), jnp.sin(ang)
        return jnp.concatenate([c, c], axis=1), jnp.concatenate([-s, s], axis=1)

    def per_token(r, c):
        r = jnp.broadcast_to(r[:, None, :], (rows, GRID_W, 2 * half))
        c = jnp.broadcast_to(c[None, :, :], (rows, GRID_W, 2 * half))
        return jnp.tile(jnp.concatenate([r, c], axis=-1).reshape(S, HEAD_DIM), (1, LANES // HEAD_DIM))

    (rc, rs), (cc, cs) = axis_tables(rows), axis_tables(GRID_W)
    return per_token(rc, cc), per_token(rs, cs)


def _moe_ffn(n2, h1, eid, wts, mod, w1, w3, w2, final_g):
    B, S, _ = h1.shape
    T = B * S
    n_blocks = (2 * T + MOE_ROWS - 1) // MOE_ROWS + N_EXPERTS
    R = n_blocks * MOE_ROWS
    eid2 = eid.reshape(T, LANES)
    rank, counts = _rank(eid2, 512)
    counts = counts[0, :N_EXPERTS]
    padded = (counts + MOE_ROWS - 1) // MOE_ROWS * MOE_ROWS
    pad_ends = jnp.cumsum(padded)
    pad_starts = pad_ends - padded
    e2 = eid2[:, :2]
    onehot = e2[:, :, None] == jnp.arange(N_EXPERTS, dtype=jnp.int32)[None, None, :]
    dest = (jnp.sum(jnp.where(onehot, pad_starts[None, None, :], 0), axis=-1) + rank[:, :2]).reshape(2 * T)
    blk_start = jnp.arange(n_blocks, dtype=jnp.int32) * MOE_ROWS
    blk_eid = jnp.minimum(jnp.sum(pad_ends[None, :] <= blk_start[:, None], axis=1), N_EXPERTS - 1).astype(jnp.int32)
    n_used = (pad_ends[-1:] // MOE_ROWS).astype(jnp.int32)
    xin = _dispatch(pad_ends.astype(jnp.int32), n_used, dest, n2.reshape(T, ROW_TILES, LANES), n_blocks)
    y_rows = _moe(blk_eid, n_used, xin.reshape(R * ROW_TILES, LANES), w1, w3, w2)
    return _combine(dest, y_rows.reshape(R, ROW_TILES, LANES), h1, wts, mod, final_g)


def _trunk(x, mod, p):
    B, S, _ = x.shape
    cos, sin = _rope_tables(S)
    q, k, v, z, xbc, dtr = _in_proj(x, mod, p["g1"], p["w_in"], p["qg"], p["kg"], cos, sin, p["seg"], tm=512)
    attn = _attention(q, k, v, p["attn_g"], tq=min(256, S), tk=min(1024, S))
    xact = _conv(xbc, p["conv_w"], p["conv_b"], tr=512)
    y_b = _ssd(xact, dtr, p["dt_bias"], p["a_log"], None, rev=True, rows=512)
    ssm = _ssd(xact, dtr, p["dt_bias"], p["a_log"], (y_b, z, p["d_skip"], p["ssm_g"]), rev=False, rows=512)
    h1, n2, eid, wts = _out_proj(attn, ssm, x, mod, p["w_out_a"], p["w_out_s"], p["g2"], p["w_r_hi"], p["w_r_lo"],
                                  p["b_r"], tm=512)
    return _moe_ffn(n2, h1, eid, wts, mod, p["w1"], p["w3"], p["w2"], p["final_g"])


def kernel(x_prompt, x_sample, c_prompt, c_sample, w_ada, b_ada, norm1_g, w_in, q_norm_g, k_norm_g, attn_out_g,
           conv_w, conv_b, a_log, dt_bias, d_skip, ssm_norm_g, w_out, norm2_g, w_rg, b_rg, w_re, b_re, w1, w3, w2,
           final_g):
    l = 0
    Bp, Bs = c_prompt.shape[0], c_sample.shape[0]
    rows = (Bp + Bs + SUBLANES - 1) // SUBLANES * SUBLANES
    c_all = jnp.zeros((rows, D_MODEL), F32).at[:Bp].set(c_prompt).at[Bp:Bp + Bs].set(c_sample)
    mod = _modulation(c_all, w_ada[l], b_ada[l].reshape(1, -1))
    mod_p = mod[:Bp].reshape(Bp, N_MOD, D_MODEL)
    mod_s = mod[Bp:Bp + Bs].reshape(Bs, N_MOD, D_MODEL)

    def lane_pad(a):
        return jnp.pad(a.reshape(1, -1), ((0, 0), (0, LANES - a.size)))

    head_id = jnp.arange(LANES) // HEAD_DIM
    w_r = jnp.concatenate([w_rg[l]] + [w_re[l, g] for g in range(N_EGROUPS)], axis=1)
    b_r = jnp.concatenate([b_rg[l]] + [b_re[l, g] for g in range(N_EGROUPS)])
    w_r = jnp.pad(w_r, ((0, 0), (0, LANES - w_r.shape[1])))
    w_r_hi = w_r.astype(BF16)
    p = {
        "g1": norm1_g[l].reshape(1, -1),
        "w_in": jnp.pad(w_in[l], ((0, 0), (0, D_IN_PAD - D_IN_PROJ))).astype(BF16),
        "qg": jnp.tile(q_norm_g[l] * (HEAD_DIM ** -0.5), N_HEADS).reshape(1, -1),
        "kg": jnp.tile(k_norm_g[l], N_KV_HEADS).reshape(1, -1),
        "seg": (head_id[:, None] == head_id[None, :]).astype(BF16),
        "attn_g": attn_out_g[l].reshape(1, -1),
        "conv_w": conv_w[l].reshape(CONV_K, D_XBC),
        "conv_b": conv_b[l].reshape(1, -1),
        "dt_bias": lane_pad(dt_bias[l]),
        "a_log": lane_pad(a_log[l]),
        "d_skip": jnp.repeat(d_skip[l], SSM_HEAD_DIM).reshape(1, -1),
        "ssm_g": ssm_norm_g[l].reshape(1, -1),
        "w_out_a": w_out[l, :D_ATTN].astype(BF16),
        "w_out_s": w_out[l, D_ATTN:].astype(BF16),
        "g2": norm2_g[l].reshape(1, -1),
        "w_r_hi": w_r_hi,
        "w_r_lo": (w_r - w_r_hi.astype(F32)).astype(BF16),
        "b_r": lane_pad(b_r),
        "w1": w1[l],
        "w3": w3[l],
        "w2": w2[l],
        "final_g": final_g.reshape(1, -1),
    }
    return _trunk(x_prompt, mod_p, p), _trunk(x_sample, mod_s, p)
```

```python
import functools

import jax
import jax.numpy as jnp
from jax import lax
from jax.experimental import pallas as pl
from jax.experimental.pallas import tpu as pltpu

F32 = jnp.float32
BF16 = jnp.bfloat16
HIGHEST = lax.Precision.HIGHEST

D_MODEL = 1024
GRID_W = 64
EPS = 1e-6
N_HEADS = 8
N_KV_HEADS = 2
HEAD_DIM = 64
D_ATTN = N_HEADS * HEAD_DIM
D_KV = N_KV_HEADS * HEAD_DIM
ROPE_THETA = 10000.0
SSM_HEADS = 8
SSM_HEAD_DIM = 64
D_SSM = SSM_HEADS * SSM_HEAD_DIM
SSM_STATE = 64
SSM_GROUPS = 2
CONV_K = 5
CHUNK = 128
D_XBC = D_SSM + 2 * SSM_GROUPS * SSM_STATE
D_IN_PROJ = D_ATTN + 2 * D_KV + 2 * D_SSM + 2 * SSM_GROUPS * SSM_STATE + 2 * SSM_HEADS
N_EGROUPS = 4
EXPERTS_PER_GROUP = 8
N_EXPERTS = N_EGROUPS * EXPERTS_PER_GROUP
D_EXPERT = 512
N_MOD = 6

LANES = 128
SUBLANES = 8
ROW_TILES = D_MODEL // LANES
D_IN_PAD = (D_IN_PROJ + LANES - 1) // LANES * LANES
SMEM_IDX = 1024
TOK_TILE = SMEM_IDX // 2
MOE_ROWS = 512
VMEM_LIMIT = 48 * 1024 * 1024

_Q0, _K0, _V0, _Z0, _X0, _DT0 = 0, D_ATTN, D_ATTN + D_KV, D_ATTN + 2 * D_KV, D_ATTN + 2 * D_KV + D_SSM, \
    D_ATTN + 2 * D_KV + D_SSM + D_XBC


def _params(sem):
    return pltpu.CompilerParams(dimension_semantics=sem, vmem_limit_bytes=VMEM_LIMIT)


def _silu(x):
    return x * jax.nn.sigmoid(x)


def _mod_kernel(c_ref, w_ref, b_ref, o_ref):
    o_ref[...] = jnp.dot(_silu(c_ref[...]), w_ref[...], precision=HIGHEST,
                         preferred_element_type=F32) + b_ref[...]


def _modulation(c, w, b):
    rows, n = c.shape[0], w.shape[1]
    tn = 1024
    return pl.pallas_call(
        _mod_kernel,
        grid=(n // tn,),
        in_specs=[pl.BlockSpec((rows, D_MODEL), lambda j: (0, 0)),
                  pl.BlockSpec((D_MODEL, tn), lambda j: (0, j)),
                  pl.BlockSpec((1, tn), lambda j: (0, j))],
        out_specs=pl.BlockSpec((rows, tn), lambda j: (0, j)),
        out_shape=jax.ShapeDtypeStruct((rows, n), F32),
        compiler_params=_params(("arbitrary",)),
        name="mod",
    )(c, w, b)


def _inproj_kernel(x_ref, mod_ref, g1_ref, w_ref, qg_ref, kg_ref, cos_ref, sin_ref, seg_ref,
                   q_ref, k_ref, v_ref, z_ref, xbc_ref, dt_ref):
    x = x_ref[0]
    ms = jnp.mean(x * x, axis=-1, keepdims=True)
    xn = x * lax.rsqrt(ms + EPS) * g1_ref[...]
    n1 = xn * (1.0 + mod_ref[0, 1:2, :]) + mod_ref[0, 0:1, :]
    proj = jnp.dot(n1.astype(BF16), w_ref[...], preferred_element_type=F32)

    cos = cos_ref[...]
    sin = sin_ref[...]
    seg = seg_ref[...]
    lane = lax.broadcasted_iota(jnp.int32, cos.shape, 1)
    low_half = (lane % 32) < 16

    def norm_rope(xc, gain):
        sq = xc * xc
        hi = sq.astype(BF16)
        lo = (sq - hi.astype(F32)).astype(BF16)
        ss = (jnp.dot(hi, seg, preferred_element_type=F32)
              + jnp.dot(lo, seg, preferred_element_type=F32))
        xg = xc * lax.rsqrt(ss * (1.0 / HEAD_DIM) + EPS) * gain
        rot = jnp.where(low_half, pltpu.roll(xg, LANES - 16, 1), pltpu.roll(xg, 16, 1))
        return xg * cos + rot * sin

    for c in range(D_ATTN // LANES):
        sl = slice(c * LANES, (c + 1) * LANES)
        q_ref[0, :, sl] = norm_rope(proj[:, _Q0 + c * LANES:_Q0 + (c + 1) * LANES], qg_ref[:, sl]).astype(BF16)
    k_ref[0] = norm_rope(proj[:, _K0:_V0], kg_ref[...]).astype(BF16)
    v = proj[:, _V0:_Z0]
    for j in range(N_KV_HEADS):
        mine = (lane >= HEAD_DIM * j) & (lane < HEAD_DIM * (j + 1))
        v_ref[0, j] = jnp.where(mine, v, 1.0).astype(BF16)
    z_ref[0] = proj[:, _Z0:_X0]
    xbc_ref[0] = proj[:, _X0:_DT0]
    dt_ref[0] = proj[:, _DT0:D_IN_PAD]


def _in_proj(x, mod, g1, w_in, qg, kg, cos, sin, seg, tm):
    B, S, _ = x.shape
    row = lambda b, i: (b, i, 0)
    const = lambda b, i: (0, 0)
    outs = [(D_ATTN, BF16), (D_KV, BF16), None, (D_SSM, F32), (D_XBC, F32), (LANES, F32)]
    out_specs = [pl.BlockSpec((1, tm, o[0]), row) if o else
                 pl.BlockSpec((1, N_KV_HEADS, tm, D_KV), lambda b, i: (b, 0, i, 0)) for o in outs]
    out_shape = [jax.ShapeDtypeStruct((B, S, o[0]), o[1]) if o else
                 jax.ShapeDtypeStruct((B, N_KV_HEADS, S, D_KV), BF16) for o in outs]
    return pl.pallas_call(
        _inproj_kernel,
        grid=(B, S // tm),
        in_specs=[pl.BlockSpec((1, tm, D_MODEL), row),
                  pl.BlockSpec((1, N_MOD, D_MODEL), lambda b, i: (b, 0, 0)),
                  pl.BlockSpec((1, D_MODEL), const),
                  pl.BlockSpec((D_MODEL, D_IN_PAD), const),
                  pl.BlockSpec((1, D_ATTN), const),
                  pl.BlockSpec((1, D_KV), const),
                  pl.BlockSpec((tm, LANES), lambda b, i: (i, 0)),
                  pl.BlockSpec((tm, LANES), lambda b, i: (i, 0)),
                  pl.BlockSpec((LANES, LANES), const)],
        out_specs=out_specs,
        out_shape=out_shape,
        compiler_params=_params(("parallel", "parallel")),
        name="in_proj",
    )(x, mod, g1, w_in, qg, kg, cos, sin, seg)


def _attn_kernel(q_ref, k_ref, v_ref, g_ref, o_ref, qs_ref, m_ref, acc_ref, *, tq, tk, nk):
    G = N_HEADS // N_KV_HEADS
    lane = lax.broadcasted_iota(jnp.int32, (tq, LANES), 1)
    qf = q_ref[0].astype(F32)
    for j in range(N_KV_HEADS):
        pieces = []
        for g in range(G):
            h = G * j + g
            chunk = qf[:, (h // 2) * LANES:(h // 2 + 1) * LANES]
            if h % 2 != j:
                chunk = pltpu.roll(chunk, HEAD_DIM, 1)
            keep = (lane >= HEAD_DIM * j) & (lane < HEAD_DIM * (j + 1))
            pieces.append(jnp.where(keep, chunk, 0.0))
        qs_ref[j] = jnp.concatenate(pieces, axis=0).astype(BF16)
    m_ref[...] = jnp.full(m_ref.shape, -jnp.inf, F32)
    acc_ref[...] = jnp.zeros(acc_ref.shape, F32)

    def body(c, carry):
        off = pl.multiple_of(c * tk, tk)
        kc = k_ref[0, pl.ds(off, tk), :]
        for j in range(N_KV_HEADS):
            s = lax.dot_general(qs_ref[j], kc, (((1,), (1,)), ((), ())), preferred_element_type=F32)
            m_prev = m_ref[j]
            m_new = jnp.maximum(m_prev, jnp.max(s, axis=1, keepdims=True))
            alpha = jnp.exp(m_prev - m_new)
            p = jnp.exp(s - jnp.concatenate([m_new] * (tk // LANES), axis=1))
            acc_ref[j] = alpha * acc_ref[j] + jnp.dot(p.astype(BF16), v_ref[0, j, pl.ds(off, tk), :],
                                                      preferred_element_type=F32)
            m_ref[j] = m_new
        return carry

    lax.fori_loop(0, nk, body, 0, unroll=2)

    o = [acc_ref[j] / pltpu.roll(acc_ref[j], HEAD_DIM, 1) for j in range(N_KV_HEADS)]

    def head(h, half):
        j, g = divmod(h, G)
        piece = o[j][g * tq:(g + 1) * tq, :]
        return pltpu.roll(piece, HEAD_DIM, 1) if j != half else piece

    of = jnp.concatenate(
        [jnp.where(lane < HEAD_DIM, head(2 * c, 0), head(2 * c + 1, 1)) for c in range(N_HEADS // 2)], axis=1)
    ms = jnp.mean(of * of, axis=-1, keepdims=True)
    o_ref[0] = (of * lax.rsqrt(ms + EPS) * g_ref[...]).astype(BF16)


def _attention(q, k, v, g, tq, tk):
    B, S, _ = q.shape
    G = N_HEADS // N_KV_HEADS
    kern = functools.partial(_attn_kernel, tq=tq, tk=tk, nk=S // tk)
    return pl.pallas_call(
        kern,
        grid=(B, S // tq),
        in_specs=[pl.BlockSpec((1, tq, D_ATTN), lambda b, i: (b, i, 0)),
                  pl.BlockSpec((1, S, D_KV), lambda b, i: (b, 0, 0)),
                  pl.BlockSpec((1, N_KV_HEADS, S, D_KV), lambda b, i: (b, 0, 0, 0)),
                  pl.BlockSpec((1, D_ATTN), lambda b, i: (0, 0))],
        out_specs=pl.BlockSpec((1, tq, D_ATTN), lambda b, i: (b, i, 0)),
        out_shape=jax.ShapeDtypeStruct((B, S, D_ATTN), BF16),
        scratch_shapes=[pltpu.VMEM((N_KV_HEADS, G * tq, LANES), BF16),
                        pltpu.VMEM((N_KV_HEADS, G * tq, LANES), F32),
                        pltpu.VMEM((N_KV_HEADS, G * tq, LANES), F32)],
        compiler_params=_params(("parallel", "parallel")),
        name="attn",
    )(q, k, v, g)


def _conv_kernel(xp_ref, xc_ref, xn_ref, w_ref, b_ref, o_ref, ext_ref, *, tr, nblk):
    i = pl.program_id(1)
    pad = CONV_K // 2
    ext_ref[0:SUBLANES, :] = jnp.where(i > 0, xp_ref[0], 0.0)
    ext_ref[SUBLANES:SUBLANES + tr, :] = xc_ref[0]
    ext_ref[SUBLANES + tr:, :] = jnp.where(i < nblk - 1, xn_ref[0], 0.0)
    acc = jnp.zeros((tr, D_XBC), F32) + b_ref[...]
    for kk in range(CONV_K):
        acc = acc + w_ref[kk:kk + 1, :] * ext_ref[SUBLANES - pad + kk:SUBLANES - pad + kk + tr, :]
    o_ref[0] = _silu(acc)


def _conv(xbc, w, b, tr):
    B, S, _ = xbc.shape
    nblk = S // tr
    per = tr // SUBLANES
    kern = functools.partial(_conv_kernel, tr=tr, nblk=nblk)
    return pl.pallas_call(
        kern,
        grid=(B, nblk),
        in_specs=[pl.BlockSpec((1, SUBLANES, D_XBC), lambda b, i: (b, jnp.maximum(i * per - 1, 0), 0)),
                  pl.BlockSpec((1, tr, D_XBC), lambda b, i: (b, i, 0)),
                  pl.BlockSpec((1, SUBLANES, D_XBC),
                               lambda b, i: (b, jnp.minimum((i + 1) * per, S // SUBLANES - 1), 0)),
                  pl.BlockSpec((CONV_K, D_XBC), lambda b, i: (0, 0)),
                  pl.BlockSpec((1, D_XBC), lambda b, i: (0, 0))],
        out_specs=pl.BlockSpec((1, tr, D_XBC), lambda b, i: (b, i, 0)),
        out_shape=jax.ShapeDtypeStruct((B, S, D_XBC), F32),
        scratch_shapes=[pltpu.VMEM((tr + 2 * SUBLANES, D_XBC), F32)],
        compiler_params=_params(("parallel", "parallel")),
        name="conv",
    )(xbc, xbc, xbc, w, b)


def _ssd_kernel(*refs, rev, rows):
    if rev:
        xbc_ref, dt_ref, dtb_ref, alog_ref, o_ref, st_ref = refs
    else:
        xbc_ref, dt_ref, dtb_ref, alog_ref, yb_ref, z_ref, dsk_ref, ng_ref, o_ref, st_ref = refs
    L = CHUNK
    d = 1 if rev else 0
    edge = 0 if rev else L - 1
    nchunk = rows // L

    @pl.when(pl.program_id(1) == 0)
    def _():
        st_ref[...] = jnp.zeros(st_ref.shape, F32)

    r_i = lax.broadcasted_iota(jnp.int32, (L, L), 0)
    c_i = lax.broadcasted_iota(jnp.int32, (L, L), 1)
    causal = (c_i >= r_i) if rev else (r_i >= c_i)
    tri = causal.astype(F32)
    low = c_i < SSM_HEAD_DIM
    a_row = -jnp.exp(alog_ref[...])

    def chunk(ci, carry):
        cc = (nchunk - 1 - ci) if rev else ci
        off = pl.multiple_of(cc * L, L)
        xbc = xbc_ref[0, pl.ds(off, L), :]
        xs = xbc[:, :D_SSM]
        bm = xbc[:, D_SSM:D_SSM + LANES]
        cm = xbc[:, D_SSM + LANES:]
        x_dt = dt_ref[0, pl.ds(off, L), :] + dtb_ref[...]
        dt_all = jnp.maximum(x_dt, 0.0) + jnp.log(1.0 + jnp.exp(-jnp.abs(x_dt)))
        a_all = dt_all * a_row
        acs = jnp.dot(tri, a_all, precision=HIGHEST, preferred_element_type=F32)
        acs_t = acs.T
        bm_b = bm.astype(BF16)
        bm_t = bm.T.astype(BF16)
        cg = [jnp.where(low, cm, 0.0).astype(BF16), jnp.where(low, 0.0, cm).astype(BF16)]
        cb = [lax.dot_general(cg[g], bm_b, (((1,), (1,)), ((), ())), preferred_element_type=F32)
              for g in range(SSM_GROUPS)]
        ys = []
        for c in range(SSM_HEADS // 2):
            g = (2 * c) // (SSM_HEADS // SSM_GROUPS)
            ja, jb = d * SSM_HEADS + 2 * c, d * SSM_HEADS + 2 * c + 1
            sl = slice(c * LANES, (c + 1) * LANES)
            dt_pair = jnp.where(low, dt_all[:, ja:ja + 1], dt_all[:, jb:jb + 1])
            acs_pair = jnp.where(low, acs[:, ja:ja + 1], acs[:, jb:jb + 1])
            xdt = xs[:, sl] * dt_pair
            xdt_b = xdt.astype(BF16)
            y_heads = []
            for jh in (ja, jb):
                seg = acs[:, jh:jh + 1] - acs_t[jh:jh + 1, :]
                lmat = jnp.exp(jnp.where(causal, seg, -jnp.inf))
                y_heads.append(jnp.dot((cb[g] * lmat).astype(BF16), xdt_b, preferred_element_type=F32))
            y = jnp.where(low, y_heads[0], y_heads[1])
            st = st_ref[:, sl]
            y = y + jnp.dot(cg[g], st.astype(BF16), preferred_element_type=F32) * jnp.exp(acs_pair)
            ys.append(y)
            edge_row = acs_pair[edge:edge + 1, :]
            xdec = (xdt * jnp.exp(edge_row - acs_pair)).astype(BF16)
            st_ref[:, sl] = jnp.exp(edge_row) * st + jnp.dot(bm_t, xdec, preferred_element_type=F32)
        y = jnp.concatenate(ys, axis=1)
        if rev:
            o_ref[0, pl.ds(off, L), :] = y
        else:
            y = y + yb_ref[0, pl.ds(off, L), :] + dsk_ref[...] * xs
            y = y * _silu(z_ref[0, pl.ds(off, L), :])
            gw = D_SSM // SSM_GROUPS
            outs = []
            for g in range(SSM_GROUPS):
                yg = y[:, g * gw:(g + 1) * gw]
                outs.append(yg * lax.rsqrt(jnp.mean(yg * yg, axis=-1, keepdims=True) + EPS))
            o_ref[0, pl.ds(off, L), :] = (jnp.concatenate(outs, axis=1) * ng_ref[...]).astype(BF16)
        return carry

    lax.fori_loop(0, nchunk, chunk, 0, unroll=2)


def _ssd(xbc, dt, dtb, alog, extra, rev, rows):
    B, S, _ = xbc.shape
    nblk = S // rows
    blk = (lambda b, i: (b, nblk - 1 - i, 0)) if rev else (lambda b, i: (b, i, 0))
    const = lambda b, i: (0, 0)
    in_specs = [pl.BlockSpec((1, rows, D_XBC), blk),
                pl.BlockSpec((1, rows, LANES), blk),
                pl.BlockSpec((1, LANES), const),
                pl.BlockSpec((1, LANES), const)]
    args = [xbc, dt, dtb, alog]
    if not rev:
        yb, z, dsk, ng = extra
        in_specs += [pl.BlockSpec((1, rows, D_SSM), blk), pl.BlockSpec((1, rows, D_SSM), blk),
                     pl.BlockSpec((1, D_SSM), const), pl.BlockSpec((1, D_SSM), const)]
        args += [yb, z, dsk, ng]
    return pl.pallas_call(
        functools.partial(_ssd_kernel, rev=rev, rows=rows),
        grid=(B, nblk),
        in_specs=in_specs,
        out_specs=pl.BlockSpec((1, rows, D_SSM), blk),
        out_shape=jax.ShapeDtypeStruct((B, S, D_SSM), F32 if rev else BF16),
        scratch_shapes=[pltpu.VMEM((SSM_GROUPS * SSM_STATE, D_SSM), F32)],
        compiler_params=_params(("parallel", "arbitrary")),
        name="ssd_bwd" if rev else "ssd_fwd",
    )(*args)


def _outproj_kernel(a_ref, s_ref, x_ref, mod_ref, wa_ref, ws_ref, g2_ref, wrh_ref, wrl_ref, br_ref,
                    h_ref, n2_ref, eid_ref, wts_ref):
    mix = (jnp.dot(a_ref[0], wa_ref[...], preferred_element_type=F32)
           + jnp.dot(s_ref[0], ws_ref[...], preferred_element_type=F32))
    h = x_ref[0] + mod_ref[0, 2:3, :] * mix
    h_ref[0] = h
    hn = h * lax.rsqrt(jnp.mean(h * h, axis=-1, keepdims=True) + EPS) * g2_ref[...]
    n2 = hn * (1.0 + mod_ref[0, 4:5, :]) + mod_ref[0, 3:4, :]
    for s in range(ROW_TILES):
        n2_ref[0, pl.ds(s, n2.shape[0], stride=ROW_TILES), :] = n2[:, s * LANES:(s + 1) * LANES]

    n_hi = n2.astype(BF16)
    n_lo = (n2 - n_hi.astype(F32)).astype(BF16)
    logits = (jnp.dot(n_hi, wrh_ref[...], preferred_element_type=F32)
              + (jnp.dot(n_lo, wrh_ref[...], preferred_element_type=F32)
                 + jnp.dot(n_hi, wrl_ref[...], preferred_element_type=F32))) + br_ref[...]
    lane = lax.broadcasted_iota(jnp.int32, logits.shape, 1)
    lane_f = lane.astype(F32)
    big = float(LANES)
    is_g = lane < N_EGROUPS
    gl = jnp.where(is_g, logits, -jnp.inf)
    gmax = jnp.max(gl, axis=1, keepdims=True)
    gidx = jnp.min(jnp.where(gl == gmax, lane_f, big), axis=1, keepdims=True)
    g_w = 1.0 / jnp.sum(jnp.where(is_g, jnp.exp(gl - gmax), 0.0), axis=1, keepdims=True)
    e_lo = N_EGROUPS + EXPERTS_PER_GROUP * gidx
    sel = (lane_f >= e_lo) & (lane_f < e_lo + EXPERTS_PER_GROUP)
    el = jnp.where(sel, logits, -jnp.inf)
    v1 = jnp.max(el, axis=1, keepdims=True)
    i1 = jnp.min(jnp.where(el == v1, lane_f, big), axis=1, keepdims=True)
    el2 = jnp.where(lane_f == i1, -jnp.inf, el)
    v2 = jnp.max(el2, axis=1, keepdims=True)
    i2 = jnp.min(jnp.where(el2 == v2, lane_f, big), axis=1, keepdims=True)
    t = jnp.exp(v2 - v1)
    den = 1.0 + t
    w_a = g_w * (1.0 / den)
    w_b = g_w * (t / den)
    eid_ref[0] = jnp.where(lane == 0, i1 - N_EGROUPS, jnp.where(lane == 1, i2 - N_EGROUPS, 0.0)).astype(jnp.int32)
    wts_ref[0] = jnp.where(lane == 0, w_a, jnp.where(lane == 1, w_b, 0.0))


def _out_proj(attn, ssm, x, mod, wa, ws, g2, wr_hi, wr_lo, br, tm):
    B, S, _ = x.shape
    row = lambda b, i: (b, i, 0)
    const = lambda b, i: (0, 0)
    return pl.pallas_call(
        _outproj_kernel,
        grid=(B, S // tm),
        in_specs=[pl.BlockSpec((1, tm, D_ATTN), row),
                  pl.BlockSpec((1, tm, D_SSM), row),
                  pl.BlockSpec((1, tm, D_MODEL), row),
                  pl.BlockSpec((1, N_MOD, D_MODEL), lambda b, i: (b, 0, 0)),
                  pl.BlockSpec((D_ATTN, D_MODEL), const),
                  pl.BlockSpec((D_SSM, D_MODEL), const),
                  pl.BlockSpec((1, D_MODEL), const),
                  pl.BlockSpec((D_MODEL, LANES), const),
                  pl.BlockSpec((D_MODEL, LANES), const),
                  pl.BlockSpec((1, LANES), const)],
        out_specs=[pl.BlockSpec((1, tm, D_MODEL), row),
                   pl.BlockSpec((1, tm * ROW_TILES, LANES), row),
                   pl.BlockSpec((1, tm, LANES), row), pl.BlockSpec((1, tm, LANES), row)],
        out_shape=[jax.ShapeDtypeStruct((B, S, D_MODEL), F32),
                   jax.ShapeDtypeStruct((B, S * ROW_TILES, LANES), F32),
                   jax.ShapeDtypeStruct((B, S, LANES), jnp.int32), jax.ShapeDtypeStruct((B, S, LANES), F32)],
        compiler_params=_params(("parallel", "parallel")),
        name="out_proj",
    )(attn, ssm, x, mod, wa, ws, g2, wr_hi, wr_lo, br)


def _rank_kernel(eid_ref, rank_ref, cnt_ref, carry_ref, *, tt):
    @pl.when(pl.program_id(0) == 0)
    def _():
        carry_ref[...] = jnp.zeros(carry_ref.shape, F32)

    eid = eid_ref[...]
    lane = lax.broadcasted_iota(jnp.int32, eid.shape, 1)
    r_i = lax.broadcasted_iota(jnp.int32, (tt, tt), 0)
    c_i = lax.broadcasted_iota(jnp.int32, (tt, tt), 1)
    before = (c_i < r_i).astype(BF16)
    carry = carry_ref[0:1, :]
    ranks = []
    for k in range(2):
        oh = lane == eid[:, k:k + 1]
        oh_f = oh.astype(F32)
        pref = jnp.dot(before, oh.astype(BF16), preferred_element_type=F32) + carry
        ranks.append(jnp.sum(oh_f * pref, axis=1, keepdims=True))
        carry = carry + jnp.sum(oh_f, axis=0, keepdims=True)
    carry_ref[0:1, :] = carry
    rank_ref[...] = jnp.where(lane == 0, ranks[0], jnp.where(lane == 1, ranks[1], 0.0)).astype(jnp.int32)
    cnt_ref[...] = jnp.broadcast_to(carry, cnt_ref.shape).astype(jnp.int32)


def _rank(eid, tt):
    T = eid.shape[0]
    return pl.pallas_call(
        functools.partial(_rank_kernel, tt=tt),
        grid=(T // tt,),
        in_specs=[pl.BlockSpec((tt, LANES), lambda i: (i, 0))],
        out_specs=[pl.BlockSpec((tt, LANES), lambda i: (i, 0)),
                   pl.BlockSpec((SUBLANES, LANES), lambda i: (0, 0))],
        out_shape=[jax.ShapeDtypeStruct((T, LANES), jnp.int32),
                   jax.ShapeDtypeStruct((SUBLANES, LANES), jnp.int32)],
        scratch_shapes=[pltpu.VMEM((SUBLANES, LANES), F32)],
        compiler_params=_params(("arbitrary",)),
        name="rank",
    )(eid)


def _row_copy(src, dst, sem):
    return pltpu.make_async_copy(src, dst, sem)


def _dispatch_kernel(pend_ref, nused_ref, dest_hbm, n2_ref, xin_hbm, idx_ref, zero_ref, idx_sem, row_sem, zero_sem,
                     *, n_blocks, n_steps):
    i = pl.program_id(0)

    @pl.when(i == 0)
    def _():
        zero_ref[...] = jnp.zeros(zero_ref.shape, F32)

        def zero_copy(blk):
            rows = pl.ds(pl.multiple_of(blk * MOE_ROWS, MOE_ROWS), MOE_ROWS)
            return pltpu.make_async_copy(zero_ref, xin_hbm.at[rows], zero_sem)

        def region_tail(fn):
            def per_expert(e, carry):
                start = jnp.where(e > 0, pend_ref[jnp.maximum(e - 1, 0)], 0)

                @pl.when(pend_ref[e] > start)
                def _():
                    fn(zero_copy(pend_ref[e] // MOE_ROWS - 1))

                return carry

            lax.fori_loop(0, N_EXPERTS, per_expert, 0)

            def per_unused(blk, carry):
                fn(zero_copy(blk))
                return carry

            lax.fori_loop(nused_ref[0], n_blocks, per_unused, 0)

        region_tail(lambda cp: cp.start())
        region_tail(lambda cp: cp.wait())

    def idx_copy(step, slot):
        return pltpu.make_async_copy(
            dest_hbm.at[pl.ds(pl.multiple_of(step * SMEM_IDX, SMEM_IDX), SMEM_IDX)], idx_ref.at[slot],
            idx_sem.at[slot])

    @pl.when(i == 0)
    def _():
        idx_copy(0, 0).start()

    def step(cur):
        idx_copy(i, cur).wait()

        @pl.when(i + 1 < n_steps)
        def _():
            idx_copy(i + 1, 1 - cur).start()

        def issue(t, carry):
            for k in range(2):
                _row_copy(n2_ref.at[t], xin_hbm.at[idx_ref[cur, 2 * t + k]], row_sem).start(priority=k)
            return carry

        lax.fori_loop(0, TOK_TILE, issue, 0, unroll=8)

    for parity in range(2):
        pl.when(i % 2 == parity)(functools.partial(step, parity))

    def drain(t, carry):
        for k in range(2):
            _row_copy(n2_ref.at[0], xin_hbm.at[0], row_sem).wait()
        return carry

    lax.fori_loop(0, TOK_TILE, drain, 0, unroll=8)


def _dispatch(pad_ends, n_used, dest, n2_rows, n_blocks):
    T = n2_rows.shape[0]
    any_spec = pl.BlockSpec(memory_space=pl.ANY)
    grid_spec = pltpu.PrefetchScalarGridSpec(
        num_scalar_prefetch=2,
        grid=(T // TOK_TILE,),
        in_specs=[any_spec, pl.BlockSpec((TOK_TILE, ROW_TILES, LANES), lambda i, pe, nu: (i, 0, 0))],
        out_specs=any_spec,
        scratch_shapes=[pltpu.SMEM((2, SMEM_IDX), jnp.int32), pltpu.VMEM((MOE_ROWS, ROW_TILES, LANES), F32),
                        pltpu.SemaphoreType.DMA((2,)), pltpu.SemaphoreType.DMA, pltpu.SemaphoreType.DMA],
    )
    return pl.pallas_call(
        functools.partial(_dispatch_kernel, n_blocks=n_blocks, n_steps=T // TOK_TILE),
        grid_spec=grid_spec,
        out_shape=jax.ShapeDtypeStruct((n_blocks * MOE_ROWS, ROW_TILES, LANES), F32),
        compiler_params=_params(("arbitrary",)),
        name="dispatch",
    )(pad_ends, n_used, dest, n2_rows)


def _moe_kernel(beid_ref, nused_ref, xin_ref, w1_ref, w3_ref, w2_ref, y_ref, w1b_ref, w3b_ref, w2b_ref):
    i = pl.program_id(0)

    @pl.when((i == 0) | (beid_ref[i] != beid_ref[jnp.maximum(i - 1, 0)]))
    def _():
        w1b_ref[...] = w1_ref[0].astype(BF16)
        w3b_ref[...] = w3_ref[0].astype(BF16)
        w2b_ref[...] = w2_ref[0].astype(BF16)

    @pl.when(i < nused_ref[0])
    def _():
        x = jnp.concatenate([xin_ref[pl.ds(s, MOE_ROWS, stride=ROW_TILES), :] for s in range(ROW_TILES)],
                            axis=1).astype(BF16)
        h1 = jnp.dot(x, w1b_ref[...], preferred_element_type=F32)
        h3 = jnp.dot(x, w3b_ref[...], preferred_element_type=F32)
        y = jnp.dot((_silu(h1) * h3).astype(BF16), w2b_ref[...], preferred_element_type=F32)
        for s in range(ROW_TILES):
            y_ref[pl.ds(s, MOE_ROWS, stride=ROW_TILES), :] = y[:, s * LANES:(s + 1) * LANES]

    @pl.when(i >= nused_ref[0])
    def _():
        y_ref[...] = jnp.zeros(y_ref.shape, F32)


def _moe(blk_eid, n_used, xin, w1, w3, w2):
    R = xin.shape[0] // ROW_TILES
    rows = MOE_ROWS * ROW_TILES
    grid_spec = pltpu.PrefetchScalarGridSpec(
        num_scalar_prefetch=2,
        grid=(R // MOE_ROWS,),
        in_specs=[pl.BlockSpec((rows, LANES), lambda i, be, nu: (jnp.minimum(i, nu[0] - 1), 0)),
                  pl.BlockSpec((1, D_MODEL, D_EXPERT), lambda i, be, nu: (be[i], 0, 0)),
                  pl.BlockSpec((1, D_MODEL, D_EXPERT), lambda i, be, nu: (be[i], 0, 0)),
                  pl.BlockSpec((1, D_EXPERT, D_MODEL), lambda i, be, nu: (be[i], 0, 0))],
        out_specs=pl.BlockSpec((rows, LANES), lambda i, be, nu: (i, 0)),
        scratch_shapes=[pltpu.VMEM((D_MODEL, D_EXPERT), BF16), pltpu.VMEM((D_MODEL, D_EXPERT), BF16),
                        pltpu.VMEM((D_EXPERT, D_MODEL), BF16)],
    )
    return pl.pallas_call(
        _moe_kernel,
        grid_spec=grid_spec,
        out_shape=jax.ShapeDtypeStruct(xin.shape, F32),
        compiler_params=_params(("arbitrary",)),
        name="moe",
    )(blk_eid, n_used, xin, w1, w3, w2)


def _combine_kernel(dest_hbm, y_hbm, h_ref, wts_ref, mod_ref, fg_ref, o_ref, idx_ref, buf_ref, idx_sem, row_sem,
                    *, tiles_per_seq, n_steps):
    n = pl.program_id(0) * tiles_per_seq + pl.program_id(1)

    def idx_copy(step, slot):
        return pltpu.make_async_copy(
            dest_hbm.at[pl.ds(pl.multiple_of(step * SMEM_IDX, SMEM_IDX), SMEM_IDX)], idx_ref.at[slot],
            idx_sem.at[slot])

    def gather(slot):
        def issue(t, carry):
            for k in range(2):
                dst = buf_ref.at[slot, k, pl.ds(pl.multiple_of(t * ROW_TILES, ROW_TILES), ROW_TILES)]
                _row_copy(y_hbm.at[idx_ref[slot, 2 * t + k]], dst, row_sem.at[slot]).start(priority=k)
            return carry

        lax.fori_loop(0, TOK_TILE, issue, 0, unroll=8)

    def step(cur):
        nxt = 1 - cur
        if cur == 0:
            @pl.when(n == 0)
            def _():
                idx_copy(0, 0).start()
                idx_copy(0, 0).wait()
                gather(0)
                if n_steps > 1:
                    idx_copy(1, 1).start()

        @pl.when(n + 1 < n_steps)
        def _():
            idx_copy(n + 1, nxt).wait()
            gather(nxt)

        @pl.when(n + 2 < n_steps)
        def _():
            idx_copy(n + 2, cur).start()

        def drain(t, carry):
            for k in range(2):
                _row_copy(y_hbm.at[0], buf_ref.at[cur, k, pl.ds(0, ROW_TILES)], row_sem.at[cur]).wait()
            return carry

        lax.fori_loop(0, TOK_TILE, drain, 0, unroll=8)

        wts = wts_ref[0]
        ffn = jnp.zeros((TOK_TILE, D_MODEL), F32)
        for k in range(2):
            yk = jnp.concatenate(
                [buf_ref[cur, k, pl.ds(s, TOK_TILE, stride=ROW_TILES), :] for s in range(ROW_TILES)], axis=1)
            ffn = ffn + wts[:, k:k + 1] * yk
        h = h_ref[0] + mod_ref[0, 5:6, :] * ffn
        o_ref[0] = h * lax.rsqrt(jnp.mean(h * h, axis=-1, keepdims=True) + EPS) * fg_ref[...]

    for parity in range(2):
        pl.when(n % 2 == parity)(functools.partial(step, parity))


def _combine(dest, y_rows, h, wts, mod, fg):
    B, S, _ = h.shape
    tps = S // TOK_TILE
    row = lambda b, i: (b, i, 0)
    any_spec = pl.BlockSpec(memory_space=pl.ANY)
    return pl.pallas_call(
        functools.partial(_combine_kernel, tiles_per_seq=tps, n_steps=B * tps),
        grid=(B, tps),
        in_specs=[any_spec, any_spec,
                  pl.BlockSpec((1, TOK_TILE, D_MODEL), row),
                  pl.BlockSpec((1, TOK_TILE, LANES), row),
                  pl.BlockSpec((1, N_MOD, D_MODEL), lambda b, i: (b, 0, 0)),
                  pl.BlockSpec((1, D_MODEL), lambda b, i: (0, 0))],
        out_specs=pl.BlockSpec((1, TOK_TILE, D_MODEL), row),
        out_shape=jax.ShapeDtypeStruct((B, S, D_MODEL), F32),
        scratch_shapes=[pltpu.SMEM((2, SMEM_IDX), jnp.int32),
                        pltpu.VMEM((2, 2, TOK_TILE * ROW_TILES, LANES), F32),
                        pltpu.SemaphoreType.DMA((2,)), pltpu.SemaphoreType.DMA((2,))],
        compiler_params=_params(("arbitrary", "arbitrary")),
        name="combine",
    )(dest, y_rows, h, wts, mod, fg)


def _rope_tables(S):
    half = HEAD_DIM // 4
    inv = ROPE_THETA ** (-jnp.arange(half, dtype=F32) / half)
    rows = S // GRID_W

    def axis_tables(n):
        ang = jnp.arange(n, dtype=F32)[:, None] * inv[None, :]
        c, s = jnp.cos(ang), jnp.sin(ang)
        return jnp.concatenate([c, c], axis=1), jnp.concatenate([-s, s], axis=1)

    def per_token(r, c):
        r = jnp.broadcast_to(r[:, None, :], (rows, GRID_W, 2 * half))
        c = jnp.broadcast_to(c[None, :, :], (rows, GRID_W, 2 * half))
        return jnp.tile(jnp.concatenate([r, c], axis=-1).reshape(S, HEAD_DIM), (1, LANES // HEAD_DIM))

    (rc, rs), (cc, cs) = axis_tables(rows), axis_tables(GRID_W)
    return per_token(rc, cc), per_token(rs, cs)


def _moe_ffn(n2, h1, eid, wts, mod, w1, w3, w2, final_g):
    B, S, _ = h1.shape
    T = B * S
    n_blocks = (2 * T + MOE_ROWS - 1) // MOE_ROWS + N_EXPERTS
    R = n_blocks * MOE_ROWS
    eid2 = eid.reshape(T, LANES)
    rank, counts = _rank(eid2, 512)
    counts = counts[0, :N_EXPERTS]
    padded = (counts + MOE_ROWS - 1) // MOE_ROWS * MOE_ROWS
    pad_ends = jnp.cumsum(padded)
    pad_starts = pad_ends - padded
    e2 = eid2[:, :2]
    onehot = e2[:, :, None] == jnp.arange(N_EXPERTS, dtype=jnp.int32)[None, None, :]
    dest = (jnp.sum(jnp.where(onehot, pad_starts[None, None, :], 0), axis=-1) + rank[:, :2]).reshape(2 * T)
    blk_start = jnp.arange(n_blocks, dtype=jnp.int32) * MOE_ROWS
    blk_eid = jnp.minimum(jnp.sum(pad_ends[None, :] <= blk_start[:, None], axis=1), N_EXPERTS - 1).astype(jnp.int32)
    n_used = (pad_ends[-1:] // MOE_ROWS).astype(jnp.int32)
    xin = _dispatch(pad_ends.astype(jnp.int32), n_used, dest, n2.reshape(T, ROW_TILES, LANES), n_blocks)
    y_rows = _moe(blk_eid, n_used, xin.reshape(R * ROW_TILES, LANES), w1, w3, w2)
    return _combine(dest, y_rows.reshape(R, ROW_TILES, LANES), h1, wts, mod, final_g)


def _trunk(x, mod, p):
    B, S, _ = x.shape
    cos, sin = _rope_tables(S)
    q, k, v, z, xbc, dtr = _in_proj(x, mod, p["g1"], p["w_in"], p["qg"], p["kg"], cos, sin, p["seg"], tm=512)
    attn = _attention(q, k, v, p["attn_g"], tq=min(256, S), tk=min(2048, S))
    xact = _conv(xbc, p["conv_w"], p["conv_b"], tr=512)
    y_b = _ssd(xact, dtr, p["dt_bias"], p["a_log"], None, rev=True, rows=512)
    ssm = _ssd(xact, dtr, p["dt_bias"], p["a_log"], (y_b, z, p["d_skip"], p["ssm_g"]), rev=False, rows=512)
    h1, n2, eid, wts = _out_proj(attn, ssm, x, mod, p["w_out_a"], p["w_out_s"], p["g2"], p["w_r_hi"], p["w_r_lo"],
                                  p["b_r"], tm=512)
    return _moe_ffn(n2, h1, eid, wts, mod, p["w1"], p["w3"], p["w2"], p["final_g"])


def kernel(x_prompt, x_sample, c_prompt, c_sample, w_ada, b_ada, norm1_g, w_in, q_norm_g, k_norm_g, attn_out_g,
           conv_w, conv_b, a_log, dt_bias, d_skip, ssm_norm_g, w_out, norm2_g, w_rg, b_rg, w_re, b_re, w1, w3, w2,
           final_g):
    l = 0
    Bp, Bs = c_prompt.shape[0], c_sample.shape[0]
    rows = (Bp + Bs + SUBLANES - 1) // SUBLANES * SUBLANES
    c_all = jnp.zeros((rows, D_MODEL), F32).at[:Bp].set(c_prompt).at[Bp:Bp + Bs].set(c_sample)
    mod = _modulation(c_all, w_ada[l], b_ada[l].reshape(1, -1))
    mod_p = mod[:Bp].reshape(Bp, N_MOD, D_MODEL)
    mod_s = mod[Bp:Bp + Bs].reshape(Bs, N_MOD, D_MODEL)

    def lane_pad(a):
        return jnp.pad(a.reshape(1, -1), ((0, 0), (0, LANES - a.size)))

    head_id = jnp.arange(LANES) // HEAD_DIM
    w_r = jnp.concatenate([w_rg[l]] + [w_re[l, g] for g in range(N_EGROUPS)], axis=1)
    b_r = jnp.concatenate([b_rg[l]] + [b_re[l, g] for g in range(N_EGROUPS)])
    w_r = jnp.pad(w_r, ((0, 0), (0, LANES - w_r.shape[1])))
    w_r_hi = w_r.astype(BF16)
    p = {
        "g1": norm1_g[l].reshape(1, -1),
        "w_in": jnp.pad(w_in[l], ((0, 0), (0, D_IN_PAD - D_IN_PROJ))).astype(BF16),
        "qg": jnp.tile(q_norm_g[l] * (HEAD_DIM ** -0.5), N_HEADS).reshape(1, -1),
        "kg": jnp.tile(k_norm_g[l], N_KV_HEADS).reshape(1, -1),
        "seg": (head_id[:, None] == head_id[None, :]).astype(BF16),
        "attn_g": attn_out_g[l].reshape(1, -1),
        "conv_w": conv_w[l].reshape(CONV_K, D_XBC),
        "conv_b": conv_b[l].reshape(1, -1),
        "dt_bias": lane_pad(dt_bias[l]),
        "a_log": lane_pad(a_log[l]),
        "d_skip": jnp.repeat(d_skip[l], SSM_HEAD_DIM).reshape(1, -1),
        "ssm_g": ssm_norm_g[l].reshape(1, -1),
        "w_out_a": w_out[l, :D_ATTN].astype(BF16),
        "w_out_s": w_out[l, D_ATTN:].astype(BF16),
        "g2": norm2_g[l].reshape(1, -1),
        "w_r_hi": w_r_hi,
        "w_r_lo": (w_r - w_r_hi.astype(F32)).astype(BF16),
        "b_r": lane_pad(b_r),
        "w1": w1[l],
        "w3": w3[l],
        "w2": w2[l],
        "final_g": final_g.reshape(1, -1),
    }
    return _trunk(x_prompt, mod_p, p), _trunk(x_sample, mod_s, p)
```
